```python
import math
import jax, jax.numpy as jnp
from jax import lax
import numpy as np

D_MODEL = 1024
BATCH = 32
SEQ = 256
DEPTH = 1
DEC_BATCH = 4
DEC_SEQ = 4096
PAST_LEN = 512

GRID_W = 64
CHUNK = 128
N_GROUPS_A = 4
D_A = 512
GROUP_A = D_A // N_GROUPS_A
N_HEADS_B = 8
HEAD_DIM_B = 64
V_DIM_B = 2 * HEAD_DIM_B
D_B = N_HEADS_B * V_DIM_B
D_QK = N_HEADS_B * 2 * HEAD_DIM_B
D_FF = 2816
CONV_W = 3
ROPE_THETA = 10000.0
EPS = 1e-6
Q_BLOCK = 128
IN_SPLITS = (D_A, 2 * D_A, 2 * D_A + D_QK, 2 * D_A + 2 * D_QK, 2 * D_A + 2 * D_QK + D_B,
             2 * D_A + 2 * D_QK + D_B + D_MODEL)
D_IN = 2 * D_A + 2 * D_QK + D_B + 2 * D_MODEL

kernel_name = "hybrid_gmlp_diffattn_prefix_dit_step"


def lambda_init(layer_idx):
    return 0.8 - 0.6 * math.exp(-0.3 * layer_idx)


def rmsnorm(x, g):
    xf = x.astype(jnp.float32)
    r = lax.rsqrt(jnp.mean(xf * xf, axis=-1, keepdims=True) + EPS)
    return (xf * r).astype(x.dtype) * g


def axial_rope(n_tok):
    rows = n_tok // GRID_W
    rr, cc = jnp.meshgrid(jnp.arange(rows), jnp.arange(GRID_W), indexing="ij")
    row = rr.reshape(-1).astype(jnp.float32)
    col = cc.reshape(-1).astype(jnp.float32)
    n_freq = HEAD_DIM_B // 4
    inv = ROPE_THETA ** (-jnp.arange(n_freq, dtype=jnp.float32) / n_freq)
    ang = jnp.concatenate([row[:, None] * inv, col[:, None] * inv], axis=-1)
    return jnp.cos(ang), jnp.sin(ang)


def apply_rope(x, cos, sin):
    half = HEAD_DIM_B // 2
    c = cos[None, :, None, None, :].astype(x.dtype)
    s = sin[None, :, None, None, :].astype(x.dtype)
    x1, x2 = x[..., :half], x[..., half:]
    return jnp.concatenate([x1 * c - x2 * s, x2 * c + x1 * s], axis=-1)


def chunk_gmlp(u, v, w_s, b_s):
    B, T, _ = v.shape
    vc = v.reshape(B, T // CHUNK, CHUNK, N_GROUPS_A, GROUP_A)
    mixed = jnp.einsum("gpq,bnqgc->bnpgc", w_s, vc) + b_s.T[None, None, :, :, None]
    return u * mixed.reshape(B, T, D_A)


def diff_attention(q, k, v, lam):
    B, T, H, _, Dh = q.shape
    nb = T // Q_BLOCK
    scale = Dh ** -0.5
    qb = q.reshape(B, nb, Q_BLOCK, H, 2, Dh).transpose(1, 0, 2, 3, 4, 5)

    def one_block(qi):
        s = jnp.einsum("bqhmd,bkhmd->bhmqk", qi, k).astype(jnp.float32) * scale
        p = jax.nn.softmax(s, axis=-1)
        a = p[:, :, 0] - lam * p[:, :, 1]
        return jnp.einsum("bhqk,bkhe->bqhe", a.astype(v.dtype), v)

    o = lax.map(one_block, qb)
    return o.transpose(1, 0, 2, 3, 4).reshape(B, T, H, v.shape[-1])


def token_mixers(h, ctx_k, ctx_v, rope, p, lam_init):
    B, T, _ = h.shape
    proj = h @ p["w_in"]
    u, va, q, k, v, ga, gb = jnp.split(proj, IN_SPLITS, axis=-1)
    a = chunk_gmlp(jax.nn.gelu(u), rmsnorm(jax.nn.gelu(va), p["g_sgu"]), p["w_s"], p["b_s"])
    q = q.reshape(B, T, N_HEADS_B, 2, HEAD_DIM_B)
    k = k.reshape(B, T, N_HEADS_B, 2, HEAD_DIM_B)
    v = v.reshape(B, T, N_HEADS_B, V_DIM_B)
    k_own = k.reshape(B, T, N_HEADS_B, 2 * HEAD_DIM_B)
    if rope is not None:
        cos, sin = rope
        q = apply_rope(q, cos, sin)
        k = apply_rope(k, cos, sin)
    if ctx_k is not None:
        S = ctx_k.shape[1]
        k = jnp.concatenate([ctx_k.reshape(B, S, N_HEADS_B, 2, HEAD_DIM_B), k], axis=1)
        v_all = jnp.concatenate([ctx_v, v], axis=1)
    else:
        v_all = v
    lq1, lk1, lq2, lk2 = (p[n].astype(jnp.float32) for n in ("lam_q1", "lam_k1", "lam_q2", "lam_k2"))
    lam = jnp.exp(jnp.sum(lq1 * lk1)) - jnp.exp(jnp.sum(lq2 * lk2)) + lam_init
    o = diff_attention(q, k, v_all, lam)
    o = rmsnorm(o, p["g_subln"]) * (1.0 - lam_init)
    b = o.reshape(B, T, D_B)
    merged = jax.nn.sigmoid(ga) * (a @ p["w_a"]) + jax.nn.sigmoid(gb) * (b @ p["w_b"])
    return merged @ p["w_o"], k_own, v


def conv_ffn(h, p):
    a, g = jnp.split(h @ p["w_up"], 2, axis=-1)
    a = lax.conv_general_dilated(a, p["conv_w"][:, None, :], window_strides=(1,), padding=((CONV_W // 2, CONV_W // 2),),
                                 dimension_numbers=("NWC", "WIO", "NWC"), feature_group_count=D_FF) + p["conv_b"]
    return (jax.nn.gelu(a) * g) @ p["w_down"]


def trunk_layer(x, cond, ctx_k, ctx_v, rope, p, lam_init):
    mod = jax.nn.silu(cond) @ p["w_ada"] + p["b_ada"]
    sh1, sc1, g1, sh2, sc2, g2 = jnp.split(mod[:, None, :], 6, axis=-1)
    h = rmsnorm(x, p["g_pre_mix"]) * (1 + sc1) + sh1
    mix, k_own, v_own = token_mixers(h, ctx_k, ctx_v, rope, p, lam_init)
    x = x + g1 * rmsnorm(mix, p["g_post_mix"])
    h = rmsnorm(x, p["g_pre_ffn"]) * (1 + sc2) + sh2
    x = x + g2 * rmsnorm(conv_ffn(h, p), p["g_post_ffn"])
    return x, k_own, v_own


def setup_inputs(seed: int = 0) -> dict:
    key = jax.random.key(seed)
    ks = iter(jax.random.split(key, 32))
    nrm = lambda shape, s: jax.random.normal(next(ks), shape, jnp.float32) * s
    gain = lambda shape: 1.0 + nrm(shape, 0.05)
    L = DEPTH
    return {
        "x_prompt": nrm((BATCH, SEQ, D_MODEL), 1.0),
        "x_sample": nrm((DEC_BATCH, DEC_SEQ, D_MODEL), 1.0),
        "c": nrm((DEC_BATCH, D_MODEL), 1.0),
        "cache_k": nrm((DEC_BATCH, L, PAST_LEN, N_HEADS_B, 2 * HEAD_DIM_B), 1.0),
        "cache_v": nrm((DEC_BATCH, L, PAST_LEN, N_HEADS_B, V_DIM_B), 1.0),
        "c_ctx": nrm((D_MODEL,), 1.0),
        "w_ada": nrm((L, D_MODEL, 6 * D_MODEL), 0.5 * D_MODEL ** -0.5),
        "b_ada": nrm((L, 6 * D_MODEL), 0.02),
        "g_pre_mix": gain((L, D_MODEL)),
        "g_post_mix": gain((L, D_MODEL)),
        "g_pre_ffn": gain((L, D_MODEL)),
        "g_post_ffn": gain((L, D_MODEL)),
        "w_in": nrm((L, D_MODEL, D_IN), D_MODEL ** -0.5),
        "g_sgu": gain((L, D_A)),
        "w_s": nrm((L, N_GROUPS_A, CHUNK, CHUNK), CHUNK ** -0.5),
        "b_s": gain((L, N_GROUPS_A, CHUNK)),
        "lam_q1": nrm((L, HEAD_DIM_B), 0.1),
        "lam_k1": nrm((L, HEAD_DIM_B), 0.1),
        "lam_q2": nrm((L, HEAD_DIM_B), 0.1),
        "lam_k2": nrm((L, HEAD_DIM_B), 0.1),
        "g_subln": gain((L, V_DIM_B)),
        "w_a": nrm((L, D_A, D_MODEL), D_A ** -0.5),
        "w_b": nrm((L, D_B, D_MODEL), D_B ** -0.5),
        "w_o": nrm((L, D_MODEL, D_MODEL), D_MODEL ** -0.5),
        "w_up": nrm((L, D_MODEL, 2 * D_FF), D_MODEL ** -0.5),
        "conv_w": nrm((L, CONV_W, D_FF), CONV_W ** -0.5),
        "conv_b": nrm((L, D_FF), 0.02),
        "w_down": nrm((L, D_FF, D_MODEL), D_FF ** -0.5),
    }


def reference(x_prompt, x_sample, c, cache_k, cache_v, c_ctx, w_ada, b_ada, g_pre_mix, g_post_mix,
              g_pre_ffn, g_post_ffn, w_in, g_sgu, w_s, b_s, lam_q1, lam_k1, lam_q2, lam_k2, g_subln,
              w_a, w_b, w_o, w_up, conv_w, conv_b, w_down):
    rope = axial_rope(x_sample.shape[1])
    yp, ys = x_prompt, x_sample
    new_k, new_v = [], []
    for l in range(DEPTH):
        p = {"w_ada": w_ada[l], "b_ada": b_ada[l], "g_pre_mix": g_pre_mix[l], "g_post_mix": g_post_mix[l],
             "g_pre_ffn": g_pre_ffn[l], "g_post_ffn": g_post_ffn[l], "w_in": w_in[l], "g_sgu": g_sgu[l],
             "w_s": w_s[l], "b_s": b_s[l], "lam_q1": lam_q1[l], "lam_k1": lam_k1[l], "lam_q2": lam_q2[l],
             "lam_k2": lam_k2[l], "g_subln": g_subln[l], "w_a": w_a[l], "w_b": w_b[l], "w_o": w_o[l],
             "w_up": w_up[l], "conv_w": conv_w[l], "conv_b": conv_b[l], "w_down": w_down[l]}
        lam_init = lambda_init(l)
        yp, k_ctx, v_ctx = trunk_layer(yp, c_ctx[None, :], None, None, None, p, lam_init)
        new_k.append(k_ctx)
        new_v.append(v_ctx)
        ys, _, _ = trunk_layer(ys, c, cache_k[:, l], cache_v[:, l], rope, p, lam_init)
    new_k_arr = jnp.stack(new_k, axis=1)
    new_v_arr = jnp.stack(new_v, axis=1)
    return (yp, ys, new_k_arr, new_v_arr)
```

```python
import functools
import math

import jax
import jax.numpy as jnp
from jax import lax
from jax.experimental import pallas as pl
from jax.experimental.pallas import tpu as pltpu

D_MODEL = 1024
GRID_W = 64
CHUNK = 128
N_GROUPS_A = 4
D_A = 512
N_HEADS_B = 8
HEAD_DIM_B = 64
V_DIM_B = 2 * HEAD_DIM_B
D_B = N_HEADS_B * V_DIM_B
D_QK = N_HEADS_B * 2 * HEAD_DIM_B
D_FF = 2816
ROPE_THETA = 10000.0
EPS = 1e-6
D_IN = 2 * D_A + 2 * D_QK + D_B + 2 * D_MODEL
OFF_U, OFF_VA, OFF_Q, OFF_K, OFF_V, OFF_GA, OFF_GB = 0, D_A, 2 * D_A, 2 * D_A + D_QK, 2 * D_A + 2 * D_QK, \
    2 * D_A + 2 * D_QK + D_B, 2 * D_A + 2 * D_QK + D_B + D_MODEL

LANES = 128
BF16_SUBLANES = 16
VMEM_LIMIT_BYTES = 56 * 1024 * 1024

BF16 = jnp.bfloat16
F32 = jnp.float32


def _lambda_init(layer_idx):
    return 0.8 - 0.6 * math.exp(-0.3 * layer_idx)


def _rms(x, g):
    r = lax.rsqrt(jnp.mean(x * x, axis=-1, keepdims=True) + EPS)
    return (x * r) * g


def _dot(a, b):
    return jnp.dot(a, b, preferred_element_type=F32)


def _const_spec(shape):
    nd = len(shape)
    return pl.BlockSpec(shape, lambda *_: (0,) * nd, pipeline_mode=pl.Buffered(1))


def _mod_spec(mod, tm, seq):
    if mod.shape[0] == 1:
        return pl.BlockSpec((None, 6, D_MODEL), lambda i: (0, 0, 0))
    tiles_per_seq = seq // tm
    return pl.BlockSpec((None, 6, D_MODEL), lambda i: (i // tiles_per_seq, 0, 0))


def _params(n_axes):
    return pltpu.CompilerParams(dimension_semantics=("arbitrary",) * n_axes,
                                vmem_limit_bytes=VMEM_LIMIT_BYTES)


def _ada_kernel(cond_ref, w_ref, b_ref, lq1_ref, lk1_ref, lq2_ref, lk2_ref, mod_ref, lam_ref, *, lam_init):
    c = cond_ref[...]
    s = (c * jax.nn.sigmoid(c)).astype(BF16)
    mod_ref[...] = _dot(s, w_ref[...].astype(BF16)) + b_ref[...]
    d1 = jnp.sum(lq1_ref[...] * lk1_ref[...], axis=-1, keepdims=True)
    d2 = jnp.sum(lq2_ref[...] * lk2_ref[...], axis=-1, keepdims=True)
    lam = jnp.exp(d1) - jnp.exp(d2) + lam_init
    lam_ref[...] = jnp.broadcast_to(lam, lam_ref.shape)


def _ada_call(cond, w_ada, b_ada, lq1, lk1, lq2, lk2, lam_init):
    rows = cond.shape[0]
    tn = D_MODEL
    vec = lambda n: pl.BlockSpec((1, n), lambda j: (0, 0))
    return pl.pallas_call(
        functools.partial(_ada_kernel, lam_init=lam_init),
        grid=(6 * D_MODEL // tn,),
        in_specs=[pl.BlockSpec((rows, D_MODEL), lambda j: (0, 0)),
                  pl.BlockSpec((D_MODEL, tn), lambda j: (0, j)),
                  pl.BlockSpec((1, tn), lambda j: (0, j)),
                  vec(HEAD_DIM_B), vec(HEAD_DIM_B), vec(HEAD_DIM_B), vec(HEAD_DIM_B)],
        out_specs=[pl.BlockSpec((rows, tn), lambda j: (0, j)),
                   pl.BlockSpec((8, LANES), lambda j: (0, 0))],
        out_shape=[jax.ShapeDtypeStruct((rows, 6 * D_MODEL), F32),
                   jax.ShapeDtypeStruct((8, LANES), F32)],
        compiler_params=_params(1),
        name="ada",
    )(cond, w_ada, b_ada, lq1, lk1, lq2, lk2)


def _inproj_kernel(*refs, tm, rope, emit_kv):
    refs = list(refs)
    x_ref, mod_ref, gpre_ref, w_ref, gsgu_ref = refs[:5]
    pos = 5
    if rope:
        cos_ref, sin_ref = refs[pos:pos + 2]
        pos += 2
    ug_ref, vn_ref, q_ref, k_ref, v_ref, sga_ref, sgb_ref = refs[pos:pos + 7]
    pos += 7
    if emit_kv:
        kf_ref, vf_ref = refs[pos:pos + 2]
        pos += 2
    h_scr = refs[pos]

    mod = mod_ref[...]
    sh1, sc1 = mod[0:1], mod[1:2]
    h_scr[...] = (_rms(x_ref[...], gpre_ref[...]) * (1.0 + sc1) + sh1).astype(BF16)

    def proj(off, width):
        return _dot(h_scr[...], w_ref[:, off:off + width])

    ug_ref[...] = jax.nn.gelu(proj(OFF_U, D_A))
    vn_ref[...] = _rms(jax.nn.gelu(proj(OFF_VA, D_A)), gsgu_ref[...]).astype(BF16)

    if rope:
        cos_t, sin_t = cos_ref[...], sin_ref[...]
        lane = lax.broadcasted_iota(jnp.int32, (tm, LANES), 1)
        first_half = (lane & (HEAD_DIM_B // 2)) == 0

    def rotary(xh):
        partner = jnp.where(first_half, pltpu.roll(xh, LANES - HEAD_DIM_B // 2, 1),
                            pltpu.roll(xh, HEAD_DIM_B // 2, 1))
        return xh * cos_t + partner * sin_t

    q = proj(OFF_Q, D_QK)
    for hb in range(N_HEADS_B):
        cols = slice(hb * LANES, (hb + 1) * LANES)
        qh = q[:, cols]
        if rope:
            qh = rotary(qh)
        q_ref[:, cols] = (qh * (HEAD_DIM_B ** -0.5)).astype(BF16)

    k = proj(OFF_K, D_QK)
    if emit_kv:
        kf_ref[...] = k
    for hb in range(N_HEADS_B):
        cols = slice(hb * LANES, (hb + 1) * LANES)
        kh = k[:, cols]
        if rope:
            kh = rotary(kh)
        k_ref[:, cols] = kh.astype(BF16)

    v = proj(OFF_V, D_B)
    if emit_kv:
        vf_ref[...] = v
    v_ref[...] = v.astype(BF16)

    sga_ref[...] = jax.nn.sigmoid(proj(OFF_GA, D_MODEL))
    sgb_ref[...] = jax.nn.sigmoid(proj(OFF_GB, D_MODEL))


def _inproj_call(x, mod, g_pre, w_in, g_sgu, rope_tabs, *, seq, tm, emit_kv):
    n = x.shape[0]
    tiles_per_seq = seq // tm
    rope = rope_tabs is not None
    row = lambda w: pl.BlockSpec((tm, w), lambda i: (i, 0))
    in_specs = [row(D_MODEL), _mod_spec(mod, tm, seq),
                _const_spec((1, D_MODEL)), _const_spec((D_MODEL, D_IN)), _const_spec((1, D_A))]
    args = [x, mod, g_pre, w_in, g_sgu]
    if rope:
        tab = pl.BlockSpec((tm, LANES), lambda i: (i % tiles_per_seq, 0))
        in_specs += [tab, tab]
        args += list(rope_tabs)
    widths = [(D_A, F32), (D_A, BF16), (D_QK, BF16), (D_QK, BF16), (D_B, BF16), (D_MODEL, F32), (D_MODEL, F32)]
    if emit_kv:
        widths += [(D_QK, F32), (D_B, F32)]
    return pl.pallas_call(
        functools.partial(_inproj_kernel, tm=tm, rope=rope, emit_kv=emit_kv),
        grid=(n // tm,),
        in_specs=in_specs,
        out_specs=[row(w) for w, _ in widths],
        out_shape=[jax.ShapeDtypeStruct((n, w), dt) for w, dt in widths],
        scratch_shapes=[pltpu.VMEM((tm, D_MODEL), BF16)],
        compiler_params=_params(1),
        name="inproj",
    )(*args)


def _attn_kernel(*refs, tq, kc, n_ctx, n_own, lam_init):
    refs = list(refs)
    q_ref, k_ref, v_ref = refs[:3]
    pos = 3
    if n_ctx:
        ck_ref, cv_ref = refs[pos:pos + 2]
        pos += 2
    lam_ref, gsub_ref, o_ref, kall, vt, sbuf = refs[pos:pos + 6]
    n_chunks = n_ctx + n_own

    @pl.when(pl.program_id(2) == 0)
    def _stage_keys():
        for c in range(n_ctx):
            rows = slice(c * kc, (c + 1) * kc)
            kall[c] = ck_ref[rows, :].astype(BF16)
            vt[c] = cv_ref[rows, :].T.astype(BF16)
        for c in range(n_own):
            rows = slice(c * kc, (c + 1) * kc)
            kall[n_ctx + c] = k_ref[rows, :]
            vt[n_ctx + c] = v_ref[rows, :].astype(F32).T.astype(BF16)

    q = q_ref[...]
    lane = lax.broadcasted_iota(jnp.int32, (tq, LANES), 1)
    zero = jnp.zeros_like(q)
    qz = jnp.concatenate([jnp.where(lane < HEAD_DIM_B, q, zero), jnp.where(lane >= HEAD_DIM_B, q, zero)], axis=0)

    def scores(c, m):
        s = lax.dot_general(kall[c], qz, (((1,), (1,)), ((), ())), preferred_element_type=F32)
        sbuf[c] = s
        return jnp.maximum(m, jnp.max(s, axis=0, keepdims=True))

    m = lax.fori_loop(0, n_chunks, scores, jnp.full((1, 2 * tq), -jnp.inf, F32))

    def weighted(c, carry):
        l, acc = carry
        p = jnp.exp(sbuf[c] - m)
        l = l + jnp.sum(p, axis=0, keepdims=True)
        acc = acc + _dot(vt[c], p.astype(BF16))
        return l, acc

    l, acc = lax.fori_loop(0, n_chunks, weighted,
                           (jnp.zeros((1, 2 * tq), F32), jnp.zeros((V_DIM_B, 2 * tq), F32)))
    rl = 1.0 / l
    lam = lam_ref[0:1, 0:1]
    o = acc[:, :tq] * rl[:, :tq] - lam * (acc[:, tq:] * rl[:, tq:])
    r = lax.rsqrt(jnp.mean(o * o, axis=0, keepdims=True) + EPS)
    y = (o * r) * gsub_ref[...] * (1.0 - lam_init)
    o_ref[...] = y.T.astype(BF16)


def _attn_call(q, k, v, ctx_k, ctx_v, lam, g_sub_col, *, tq, kc, lam_init):
    b, t, _ = q.shape
    n_own = t // kc
    n_ctx = 0 if ctx_k is None else ctx_k.shape[1] // kc
    n_chunks = n_own + n_ctx
    head_rows = lambda rows: pl.BlockSpec((None, rows, LANES), lambda bi, h, qi: (bi, 0, h))
    q_spec = pl.BlockSpec((None, tq, LANES), lambda bi, h, qi: (bi, qi, h))
    in_specs = [q_spec, head_rows(t), head_rows(t)]
    args = [q, k, v]
    if n_ctx:
        in_specs += [head_rows(ctx_k.shape[1]), head_rows(ctx_v.shape[1])]
        args += [ctx_k, ctx_v]
    in_specs += [pl.BlockSpec((8, LANES), lambda bi, h, qi: (0, 0)),
                 pl.BlockSpec((V_DIM_B, 1), lambda bi, h, qi: (0, 0))]
    args += [lam, g_sub_col]
    return pl.pallas_call(
        functools.partial(_attn_kernel, tq=tq, kc=kc, n_ctx=n_ctx, n_own=n_own, lam_init=lam_init),
        grid=(b, N_HEADS_B, t // tq),
        in_specs=in_specs,
        out_specs=q_spec,
        out_shape=jax.ShapeDtypeStruct((b, t, D_B), BF16),
        scratch_shapes=[pltpu.VMEM((n_chunks, kc, LANES), BF16),
                        pltpu.VMEM((n_chunks, V_DIM_B, kc), BF16),
                        pltpu.VMEM((n_chunks, kc, 2 * tq), F32)],
        compiler_params=_params(3),
        name="attn",
    )(*args)


def _mix_kernel(ug_ref, vn_ref, o_ref, sga_ref, sgb_ref, x_ref, mod_ref, ws_ref, bs_ref, wa_ref, wb_ref, wo_ref,
                gpost_ref, x1_ref, a_scr, *, tm):
    for j in range(tm // CHUNK):
        rows = slice(j * CHUNK, (j + 1) * CHUNK)
        for g in range(N_GROUPS_A):
            cols = slice(g * LANES, (g + 1) * LANES)
            mixed = _dot(ws_ref[g], vn_ref[rows, cols]) + bs_ref[:, g:g + 1]
            a_scr[rows, cols] = (ug_ref[rows, cols] * mixed).astype(BF16)
    merged = sga_ref[...] * _dot(a_scr[...], wa_ref[...]) + sgb_ref[...] * _dot(o_ref[...], wb_ref[...])
    mix = _dot(merged.astype(BF16), wo_ref[...])
    g1 = mod_ref[2:3, :]
    x1_ref[...] = x_ref[...] + g1 * _rms(mix, gpost_ref[...])


def _mix_call(ug, vn, o, sga, sgb, x, mod, w_s, b_s_t, w_a, w_b, w_o, g_post, *, seq, tm):
    n = x.shape[0]
    row = lambda w: pl.BlockSpec((tm, w), lambda i: (i, 0))
    return pl.pallas_call(
        functools.partial(_mix_kernel, tm=tm),
        grid=(n // tm,),
        in_specs=[row(D_A), row(D_A), row(D_B), row(D_MODEL), row(D_MODEL), row(D_MODEL),
                  _mod_spec(mod, tm, seq),
                  _const_spec((N_GROUPS_A, CHUNK, CHUNK)), _const_spec((CHUNK, N_GROUPS_A)),
                  _const_spec((D_A, D_MODEL)), _const_spec((D_B, D_MODEL)), _const_spec((D_MODEL, D_MODEL)),
                  _const_spec((1, D_MODEL))],
        out_specs=row(D_MODEL),
        out_shape=jax.ShapeDtypeStruct((n, D_MODEL), F32),
        scratch_shapes=[pltpu.VMEM((tm, D_A), BF16)],
        compiler_params=_params(1),
        name="mix",
    )(ug, vn, o, sga, sgb, x, mod, w_s, b_s_t, w_a, w_b, w_o, g_post)


FFN_HALO = BF16_SUBLANES


def _ffn_kernel(*refs, tm, seq, halo, n_col_chunks):
    refs = list(refs)
    x_ref = refs[0]
    pos = 1
    if halo:
        xp_ref, xn_ref = refs[1:3]
        pos = 3
    mod_ref, gpre_ref, wup_ref, cw_ref, cb_ref, wdn_ref, gpost_ref, out_ref, hbuf = refs[pos:pos + 9]

    mod = mod_ref[...]
    sh2, sc2, g2 = mod[3:4], mod[4:5], mod[5:6]

    def pre(xv):
        return (_rms(xv, gpre_ref[...]) * (1.0 + sc2) + sh2).astype(BF16)

    x = x_ref[...]
    lo = FFN_HALO if halo else 0
    rows_all = tm + 2 * lo
    if halo:
        hbuf[0:lo] = pre(xp_ref[...])
        hbuf[lo + tm:rows_all] = pre(xn_ref[...])
    hbuf[lo:lo + tm] = pre(x)

    t_in_seq = (pl.program_id(0) * tm + lax.broadcasted_iota(jnp.int32, (tm, 1), 0)) & (seq - 1)
    is_first = t_in_seq == 0
    is_last = t_in_seq == seq - 1

    cw_chunk = D_FF // n_col_chunks
    acc = None
    for c in range(n_col_chunks):
        ca = c * cw_chunk
        a = _dot(hbuf[...], wup_ref[:, ca:ca + cw_chunk])
        gate = _dot(hbuf[lo:lo + tm], wup_ref[:, D_FF + ca:D_FF + ca + cw_chunk])
        a_prev = pltpu.roll(a, 1, 0)[lo:lo + tm]
        a_next = pltpu.roll(a, rows_all - 1, 0)[lo:lo + tm]
        a_mid = a[lo:lo + tm]
        a_prev = jnp.where(is_first, 0.0, a_prev)
        a_next = jnp.where(is_last, 0.0, a_next)
        cw = cw_ref[:, ca:ca + cw_chunk]
        conv = cw[0:1] * a_prev + cw[1:2] * a_mid + cw[2:3] * a_next + cb_ref[:, ca:ca + cw_chunk]
        act = (jax.nn.gelu(conv) * gate).astype(BF16)
        y = _dot(act, wdn_ref[ca:ca + cw_chunk, :])
        acc = y if acc is None else acc + y
    out_ref[...] = x + g2 * _rms(acc, gpost_ref[...])


def _ffn_call(x1, mod, g_pre, w_up, conv_w, conv_b, w_down, g_post, *, seq, tm, n_col_chunks):
    n = x1.shape[0]
    halo = tm % seq != 0
    row = pl.BlockSpec((tm, D_MODEL), lambda i: (i, 0))
    in_specs = [row]
    args = [x1]
    if halo:
        hb = tm // FFN_HALO
        last = n // FFN_HALO - 1
        in_specs += [pl.BlockSpec((FFN_HALO, D_MODEL), lambda i: (jnp.maximum(i * hb - 1, 0), 0)),
                     pl.BlockSpec((FFN_HALO, D_MODEL), lambda i: (jnp.minimum((i + 1) * hb, last), 0))]
        args += [x1, x1]
    in_specs += [_mod_spec(mod, tm, seq),
                 _const_spec((1, D_MODEL)), _const_spec((D_MODEL, 2 * D_FF)), _const_spec((3, D_FF)),
                 _const_spec((1, D_FF)), _const_spec((D_FF, D_MODEL)), _const_spec((1, D_MODEL))]
    args += [mod, g_pre, w_up, conv_w, conv_b, w_down, g_post]
    rows_all = tm + (2 * FFN_HALO if halo else 0)
    return pl.pallas_call(
        functools.partial(_ffn_kernel, tm=tm, seq=seq, halo=halo, n_col_chunks=n_col_chunks),
        grid=(n // tm,),
        in_specs=in_specs,
        out_specs=row,
        out_shape=jax.ShapeDtypeStruct((n, D_MODEL), F32),
        scratch_shapes=[pltpu.VMEM((rows_all, D_MODEL), BF16)],
        compiler_params=_params(1),
        name="ffn",
    )(*args)


def _rope_tables(n_tok):
    pos = jnp.arange(n_tok)
    row = (pos // GRID_W).astype(F32)
    col = (pos % GRID_W).astype(F32)
    n_freq = HEAD_DIM_B // 4
    inv = ROPE_THETA ** (-jnp.arange(n_freq, dtype=F32) / n_freq)
    ang = jnp.concatenate([row[:, None] * inv, col[:, None] * inv], axis=-1)
    cos, sin = jnp.cos(ang), jnp.sin(ang)
    cos_t = jnp.tile(cos, (1, LANES // (HEAD_DIM_B // 2)))
    sin_t = jnp.tile(jnp.concatenate([-sin, sin], axis=-1), (1, LANES // HEAD_DIM_B))
    return cos_t, sin_t


def _group(x, mod, p, lam, rope_tabs, ctx_k, ctx_v, lam_init, *, tm_proj, tm_mix, tm_ffn, tq, kc, emit_kv):
    b, t, _ = x.shape
    xf = x.reshape(b * t, D_MODEL)
    outs = _inproj_call(xf, mod, p["g_pre_mix"], p["w_in"], p["g_sgu"], rope_tabs, seq=t, tm=tm_proj,
                        emit_kv=emit_kv)
    ug, vn, q, k, v, sga, sgb = outs[:7]
    o = _attn_call(q.reshape(b, t, D_QK), k.reshape(b, t, D_QK), v.reshape(b, t, D_B), ctx_k, ctx_v, lam,
                   p["g_subln_col"], tq=tq, kc=kc, lam_init=lam_init)
    x1 = _mix_call(ug, vn, o.reshape(b * t, D_B), sga, sgb, xf, mod, p["w_s"], p["b_s_t"], p["w_a"], p["w_b"],
                   p["w_o"], p["g_post_mix"], seq=t, tm=tm_mix)
    y = _ffn_call(x1, mod, p["g_pre_ffn"], p["w_up"], p["conv_w"], p["conv_b"], p["w_down"], p["g_post_ffn"],
                  seq=t, tm=tm_ffn, n_col_chunks=2)
    kv = outs[7:] if emit_kv else None
    return y.reshape(b, t, D_MODEL), kv


def kernel(x_prompt, x_sample, c, cache_k, cache_v, c_ctx, w_ada, b_ada, g_pre_mix, g_post_mix, g_pre_ffn,
           g_post_ffn, w_in, g_sgu, w_s, b_s, lam_q1, lam_k1, lam_q2, lam_k2, g_subln, w_a, w_b, w_o, w_up,
           conv_w, conv_b, w_down):
    depth = w_in.shape[0]
    assert depth == 1, "single-layer configuration"
    l = 0
    lam_init = _lambda_init(l)
    nb, ns = x_prompt.shape[0], x_sample.shape[0]
    t_s = x_sample.shape[1]

    p = {
        "g_pre_mix": g_pre_mix[l][None], "g_post_mix": g_post_mix[l][None],
        "g_pre_ffn": g_pre_ffn[l][None], "g_post_ffn": g_post_ffn[l][None],
        "w_in": w_in[l].astype(BF16), "g_sgu": g_sgu[l][None],
        "w_s": w_s[l].astype(BF16), "b_s_t": b_s[l].T,
        "g_subln_col": g_subln[l][:, None],
        "w_a": w_a[l].astype(BF16), "w_b": w_b[l].astype(BF16), "w_o": w_o[l].astype(BF16),
        "w_up": w_up[l].astype(BF16), "conv_w": conv_w[l], "conv_b": conv_b[l][None],
        "w_down": w_down[l].astype(BF16),
    }

    cond = jnp.concatenate([c_ctx[None], c, jnp.zeros((8 - 1 - ns, D_MODEL), F32)], axis=0)
    mod, lam = _ada_call(cond, w_ada[l], b_ada[l][None], lam_q1[l][None], lam_k1[l][None], lam_q2[l][None],
                         lam_k2[l][None], lam_init)
    mod = mod.reshape(8, 6, D_MODEL)
    mod_ctx, mod_s = mod[0:1], mod[1:1 + ns]

    yp, kv = _group(x_prompt, mod_ctx, p, lam, None, None, None, lam_init,
                    tm_proj=256, tm_mix=256, tm_ffn=512, tq=256, kc=256, emit_kv=True)
    past = cache_k.shape[2]
    ctx_k = cache_k[:, l].reshape(ns, past, D_QK)
    ctx_v = cache_v[:, l].reshape(ns, past, D_B)
    ys, _ = _group(x_sample, mod_s, p, lam, _rope_tables(t_s), ctx_k, ctx_v, lam_init,
                   tm_proj=256, tm_mix=256, tm_ffn=512, tq=256, kc=512, emit_kv=False)

    seq = x_prompt.shape[1]
    new_k = kv[0].reshape(nb, 1, seq, N_HEADS_B, 2 * HEAD_DIM_B)
    new_v = kv[1].reshape(nb, 1, seq, N_HEADS_B, V_DIM_B)
    return (yp, ys, new_k, new_v)
```

```python
import functools
import math

import jax
import jax.numpy as jnp
from jax import lax
from jax.experimental import pallas as pl
from jax.experimental.pallas import tpu as pltpu

D_MODEL = 1024
GRID_W = 64
CHUNK = 128
N_GROUPS_A = 4
D_A = 512
N_HEADS_B = 8
HEAD_DIM_B = 64
V_DIM_B = 2 * HEAD_DIM_B
D_B = N_HEADS_B * V_DIM_B
D_QK = N_HEADS_B * 2 * HEAD_DIM_B
D_FF = 2816
ROPE_THETA = 10000.0
EPS = 1e-6
D_IN = 2 * D_A + 2 * D_QK + D_B + 2 * D_MODEL
OFF_U, OFF_VA, OFF_Q, OFF_K, OFF_V, OFF_GA, OFF_GB = 0, D_A, 2 * D_A, 2 * D_A + D_QK, 2 * D_A + 2 * D_QK, \
    2 * D_A + 2 * D_QK + D_B, 2 * D_A + 2 * D_QK + D_B + D_MODEL

Q_SCALE = HEAD_DIM_B ** -0.5 * math.log2(math.e)

LANES = 128
BF16_SUBLANES = 16
VMEM_LIMIT_BYTES = 56 * 1024 * 1024

BF16 = jnp.bfloat16
F32 = jnp.float32


def _lambda_init(layer_idx):
    return 0.8 - 0.6 * math.exp(-0.3 * layer_idx)


def _rms(x, g):
    r = lax.rsqrt(jnp.mean(x * x, axis=-1, keepdims=True) + EPS)
    return (x * r) * g


def _dot(a, b):
    return jnp.dot(a, b, preferred_element_type=F32)


def _const_spec(shape):
    nd = len(shape)
    return pl.BlockSpec(shape, lambda *_: (0,) * nd, pipeline_mode=pl.Buffered(1))


def _mod_spec(mod, tm, seq):
    if mod.shape[0] == 1:
        return pl.BlockSpec((None, 6, D_MODEL), lambda i: (0, 0, 0))
    tiles_per_seq = seq // tm
    return pl.BlockSpec((None, 6, D_MODEL), lambda i: (i // tiles_per_seq, 0, 0))


def _params(n_axes):
    return pltpu.CompilerParams(dimension_semantics=("arbitrary",) * n_axes,
                                vmem_limit_bytes=VMEM_LIMIT_BYTES)


def _ada_kernel(cond_ref, w_ref, b_ref, lq1_ref, lk1_ref, lq2_ref, lk2_ref, mod_ref, lam_ref, *, lam_init):
    c = cond_ref[...]
    s = (c * jax.nn.sigmoid(c)).astype(BF16)
    mod_ref[...] = _dot(s, w_ref[...].astype(BF16)) + b_ref[...]
    d1 = jnp.sum(lq1_ref[...] * lk1_ref[...], axis=-1, keepdims=True)
    d2 = jnp.sum(lq2_ref[...] * lk2_ref[...], axis=-1, keepdims=True)
    lam = jnp.exp(d1) - jnp.exp(d2) + lam_init
    lam_ref[...] = jnp.broadcast_to(lam, lam_ref.shape)


def _ada_call(cond, w_ada, b_ada, lq1, lk1, lq2, lk2, lam_init):
    rows = cond.shape[0]
    tn = D_MODEL
    vec = lambda n: pl.BlockSpec((1, n), lambda j: (0, 0))
    return pl.pallas_call(
        functools.partial(_ada_kernel, lam_init=lam_init),
        grid=(6 * D_MODEL // tn,),
        in_specs=[pl.BlockSpec((rows, D_MODEL), lambda j: (0, 0)),
                  pl.BlockSpec((D_MODEL, tn), lambda j: (0, j)),
                  pl.BlockSpec((1, tn), lambda j: (0, j)),
                  vec(HEAD_DIM_B), vec(HEAD_DIM_B), vec(HEAD_DIM_B), vec(HEAD_DIM_B)],
        out_specs=[pl.BlockSpec((rows, tn), lambda j: (0, j)),
                   pl.BlockSpec((8, LANES), lambda j: (0, 0))],
        out_shape=[jax.ShapeDtypeStruct((rows, 6 * D_MODEL), F32),
                   jax.ShapeDtypeStruct((8, LANES), F32)],
        compiler_params=_params(1),
        name="ada",
    )(cond, w_ada, b_ada, lq1, lk1, lq2, lk2)


def _inproj_kernel(*refs, tm, rope, emit_kv):
    refs = list(refs)
    x_ref, mod_ref, gpre_ref, w_ref, gsgu_ref = refs[:5]
    pos = 5
    if rope:
        cos_ref, sin_ref = refs[pos:pos + 2]
        pos += 2
    ug_ref, vn_ref, q_ref, k_ref, v_ref, sga_ref, sgb_ref = refs[pos:pos + 7]
    pos += 7
    if emit_kv:
        kf_ref, vf_ref = refs[pos:pos + 2]
        pos += 2
    h_scr = refs[pos]

    mod = mod_ref[...]
    sh1, sc1 = mod[0:1], mod[1:2]
    h_scr[...] = (_rms(x_ref[...], gpre_ref[...]) * (1.0 + sc1) + sh1).astype(BF16)

    def proj(off, width):
        return _dot(h_scr[...], w_ref[:, off:off + width])

    ug_ref[...] = jax.nn.gelu(proj(OFF_U, D_A))
    vn_ref[...] = _rms(jax.nn.gelu(proj(OFF_VA, D_A)), gsgu_ref[...]).astype(BF16)

    if rope:
        cos_t, sin_t = cos_ref[...], sin_ref[...]
        lane = lax.broadcasted_iota(jnp.int32, (tm, LANES), 1)
        first_half = (lane & (HEAD_DIM_B // 2)) == 0

    def rotary(xh):
        partner = jnp.where(first_half, pltpu.roll(xh, LANES - HEAD_DIM_B // 2, 1),
                            pltpu.roll(xh, HEAD_DIM_B // 2, 1))
        return xh * cos_t + partner * sin_t

    q = proj(OFF_Q, D_QK)
    for hb in range(N_HEADS_B):
        cols = slice(hb * LANES, (hb + 1) * LANES)
        qh = q[:, cols]
        if rope:
            qh = rotary(qh)
        q_ref[:, cols] = (qh * Q_SCALE).astype(BF16)

    k = proj(OFF_K, D_QK)
    if emit_kv:
        kf_ref[...] = k
    for hb in range(N_HEADS_B):
        cols = slice(hb * LANES, (hb + 1) * LANES)
        kh = k[:, cols]
        if rope:
            kh = rotary(kh)
        k_ref[:, cols] = kh.astype(BF16)

    v = proj(OFF_V, D_B)
    if emit_kv:
        vf_ref[...] = v
    v_ref[...] = v.astype(BF16)

    sga_ref[...] = jax.nn.sigmoid(proj(OFF_GA, D_MODEL))
    sgb_ref[...] = jax.nn.sigmoid(proj(OFF_GB, D_MODEL))


def _inproj_call(x, mod, g_pre, w_in, g_sgu, rope_tabs, *, seq, tm, emit_kv):
    n = x.shape[0]
    tiles_per_seq = seq // tm
    rope = rope_tabs is not None
    row = lambda w: pl.BlockSpec((tm, w), lambda i: (i, 0))
    in_specs = [row(D_MODEL), _mod_spec(mod, tm, seq),
                _const_spec((1, D_MODEL)), _const_spec((D_MODEL, D_IN)), _const_spec((1, D_A))]
    args = [x, mod, g_pre, w_in, g_sgu]
    if rope:
        tab = pl.BlockSpec((tm, LANES), lambda i: (i % tiles_per_seq, 0))
        in_specs += [tab, tab]
        args += list(rope_tabs)
    widths = [(D_A, F32), (D_A, BF16), (D_QK, BF16), (D_QK, BF16), (D_B, BF16), (D_MODEL, F32), (D_MODEL, F32)]
    if emit_kv:
        widths += [(D_QK, F32), (D_B, F32)]
    return pl.pallas_call(
        functools.partial(_inproj_kernel, tm=tm, rope=rope, emit_kv=emit_kv),
        grid=(n // tm,),
        in_specs=in_specs,
        out_specs=[row(w) for w, _ in widths],
        out_shape=[jax.ShapeDtypeStruct((n, w), dt) for w, dt in widths],
        scratch_shapes=[pltpu.VMEM((tm, D_MODEL), BF16)],
        compiler_params=_params(1),
        name="inproj",
    )(*args)


def _two_map_queries(q):
    lane = lax.broadcasted_iota(jnp.int32, q.shape, 1)
    zero = jnp.zeros_like(q)
    return jnp.concatenate([jnp.where(lane < HEAD_DIM_B, q, zero), jnp.where(lane >= HEAD_DIM_B, q, zero)], axis=0)


def _scores_t(k, qz):
    return lax.dot_general(k, qz, (((1,), (1,)), ((), ())), preferred_element_type=F32)


def _diff_head_out(acc, l, lam, g_col, tq, lam_init):
    rl = 1.0 / l
    o = acc[:, :tq] * rl[:, :tq] - lam * (acc[:, tq:] * rl[:, tq:])
    r = lax.rsqrt(jnp.mean(o * o, axis=0, keepdims=True) + EPS)
    y = (o * r) * g_col * (1.0 - lam_init)
    return y.T.astype(BF16)


def _attn_kernel(q_ref, k_ref, v_ref, ck_ref, cv_ref, lam_ref, gsub_ref, o_ref, kall, vt, s_even, s_odd, mbuf, *,
                 tq, kc, n_ctx, n_own, unroll, lam_init):
    n_chunks = n_ctx + n_own
    j = pl.program_id(2)

    @pl.when(j == 0)
    def _stage_keys():
        for c in range(n_ctx):
            rows = slice(c * kc, (c + 1) * kc)
            kall[c] = ck_ref[rows, :].astype(BF16)
            vt[c] = cv_ref[rows, :].T.astype(BF16)
        for c in range(n_own):
            rows = slice(c * kc, (c + 1) * kc)
            kall[n_ctx + c] = k_ref[rows, :]
            vt[n_ctx + c] = v_ref[rows, :].astype(F32).T.astype(BF16)
        s_odd[...] = jnp.zeros(s_odd.shape, F32)
        mbuf[1] = jnp.zeros(mbuf.shape[1:], F32)

    def step(s_cur, s_prev, cur):
        qz = _two_map_queries(q_ref[...])
        m_prev = mbuf[1 - cur]

        def chunk(c, carry):
            m, l, acc = carry
            s = _scores_t(kall[c], qz)
            s_cur[c] = s
            m = jnp.maximum(m, jnp.max(s, axis=0, keepdims=True))
            p = jnp.exp2(s_prev[c] - m_prev)
            l = l + jnp.sum(p, axis=0, keepdims=True)
            acc = acc + _dot(vt[c], p.astype(BF16))
            return m, l, acc

        m, l, acc = lax.fori_loop(0, n_chunks, chunk,
                                  (jnp.full((1, 2 * tq), -jnp.inf, F32), jnp.zeros((1, 2 * tq), F32),
                                   jnp.zeros((V_DIM_B, 2 * tq), F32)), unroll=unroll)
        mbuf[cur] = m
        o_ref[...] = _diff_head_out(acc, l, lam_ref[0:1, 0:1], gsub_ref[...], tq, lam_init)

    pl.when(j % 2 == 0)(lambda: step(s_even, s_odd, 0))
    pl.when(j % 2 == 1)(lambda: step(s_odd, s_even, 1))


def _attn_call(q, k, v, ctx_k, ctx_v, lam, g_sub_col, *, tq, kc, lam_init):
    b, t, _ = q.shape
    n_q = t // tq
    n_own = t // kc
    n_ctx = ctx_k.shape[1] // kc
    n_chunks = n_own + n_ctx
    head_rows = lambda rows: pl.BlockSpec((None, rows, LANES), lambda bi, h, j: (bi, 0, h))
    return pl.pallas_call(
        functools.partial(_attn_kernel, tq=tq, kc=kc, n_ctx=n_ctx, n_own=n_own, unroll=True, lam_init=lam_init),
        grid=(b, N_HEADS_B, n_q + 1),
        in_specs=[pl.BlockSpec((None, tq, LANES), lambda bi, h, j: (bi, jnp.minimum(j, n_q - 1), h)),
                  head_rows(t), head_rows(t), head_rows(ctx_k.shape[1]), head_rows(ctx_v.shape[1]),
                  pl.BlockSpec((8, LANES), lambda bi, h, j: (0, 0)),
                  pl.BlockSpec((V_DIM_B, 1), lambda bi, h, j: (0, 0))],
        out_specs=pl.BlockSpec((None, tq, LANES), lambda bi, h, j: (bi, jnp.maximum(j - 1, 0), h)),
        out_shape=jax.ShapeDtypeStruct((b, t, D_B), BF16),
        scratch_shapes=[pltpu.VMEM((n_chunks, kc, LANES), BF16),
                        pltpu.VMEM((n_chunks, V_DIM_B, kc), BF16),
                        pltpu.VMEM((n_chunks, kc, 2 * tq), F32),
                        pltpu.VMEM((n_chunks, kc, 2 * tq), F32),
                        pltpu.VMEM((2, 1, 2 * tq), F32)],
        compiler_params=_params(3),
        name="attn",
    )(q, k, v, ctx_k, ctx_v, lam, g_sub_col)


def _attn_seq_kernel(q_ref, k_ref, v_ref, lam_ref, gsub_ref, o_ref, *, t, lam_init):
    lam = lam_ref[0:1, 0:1]
    g_col = gsub_ref[...]
    for h in range(N_HEADS_B):
        cols = slice(h * LANES, (h + 1) * LANES)
        s = _scores_t(k_ref[:, cols], _two_map_queries(q_ref[:, cols]))
        p = jnp.exp2(s - jnp.max(s, axis=0, keepdims=True))
        l = jnp.sum(p, axis=0, keepdims=True)
        v_t = v_ref[:, cols].astype(F32).T.astype(BF16)
        acc = _dot(v_t, p.astype(BF16))
        o_ref[:, cols] = _diff_head_out(acc, l, lam, g_col, t, lam_init)


def _attn_seq_call(q, k, v, lam, g_sub_col, *, lam_init):
    b, t, _ = q.shape
    seq_block = pl.BlockSpec((None, t, D_B), lambda bi: (bi, 0, 0))
    return pl.pallas_call(
        functools.partial(_attn_seq_kernel, t=t, lam_init=lam_init),
        grid=(b,),
        in_specs=[seq_block, seq_block, seq_block,
                  pl.BlockSpec((8, LANES), lambda bi: (0, 0)),
                  pl.BlockSpec((V_DIM_B, 1), lambda bi: (0, 0))],
        out_specs=seq_block,
        out_shape=jax.ShapeDtypeStruct((b, t, D_B), BF16),
        compiler_params=_params(1),
        name="attn_seq",
    )(q, k, v, lam, g_sub_col)


def _mix_kernel(ug_ref, vn_ref, o_ref, sga_ref, sgb_ref, x_ref, mod_ref, ws_ref, bs_ref, wa_ref, wb_ref, wo_ref,
                gpost_ref, x1_ref, a_scr, *, tm):
    for j in range(tm // CHUNK):
        rows = slice(j * CHUNK, (j + 1) * CHUNK)
        for g in range(N_GROUPS_A):
            cols = slice(g * LANES, (g + 1) * LANES)
            mixed = _dot(ws_ref[g], vn_ref[rows, cols]) + bs_ref[:, g:g + 1]
            a_scr[rows, cols] = (ug_ref[rows, cols] * mixed).astype(BF16)
    merged = sga_ref[...] * _dot(a_scr[...], wa_ref[...]) + sgb_ref[...] * _dot(o_ref[...], wb_ref[...])
    mix = _dot(merged.astype(BF16), wo_ref[...])
    g1 = mod_ref[2:3, :]
    x1_ref[...] = x_ref[...] + g1 * _rms(mix, gpost_ref[...])


def _mix_call(ug, vn, o, sga, sgb, x, mod, w_s, b_s_t, w_a, w_b, w_o, g_post, *, seq, tm):
    n = x.shape[0]
    row = lambda w: pl.BlockSpec((tm, w), lambda i: (i, 0))
    return pl.pallas_call(
        functools.partial(_mix_kernel, tm=tm),
        grid=(n // tm,),
        in_specs=[row(D_A), row(D_A), row(D_B), row(D_MODEL), row(D_MODEL), row(D_MODEL),
                  _mod_spec(mod, tm, seq),
                  _const_spec((N_GROUPS_A, CHUNK, CHUNK)), _const_spec((CHUNK, N_GROUPS_A)),
                  _const_spec((D_A, D_MODEL)), _const_spec((D_B, D_MODEL)), _const_spec((D_MODEL, D_MODEL)),
                  _const_spec((1, D_MODEL))],
        out_specs=row(D_MODEL),
        out_shape=jax.ShapeDtypeStruct((n, D_MODEL), F32),
        scratch_shapes=[pltpu.VMEM((tm, D_A), BF16)],
        compiler_params=_params(1),
        name="mix",
    )(ug, vn, o, sga, sgb, x, mod, w_s, b_s_t, w_a, w_b, w_o, g_post)


FFN_HALO = BF16_SUBLANES


def _ffn_kernel(*refs, tm, seq, halo, n_col_chunks):
    refs = list(refs)
    x_ref = refs[0]
    pos = 1
    if halo:
        xp_ref, xn_ref = refs[1:3]
        pos = 3
    mod_ref, gpre_ref, wup_ref, cw_ref, cb_ref, wdn_ref, gpost_ref, out_ref, hbuf = refs[pos:pos + 9]

    mod = mod_ref[...]
    sh2, sc2, g2 = mod[3:4], mod[4:5], mod[5:6]

    def pre(xv):
        return (_rms(xv, gpre_ref[...]) * (1.0 + sc2) + sh2).astype(BF16)

    x = x_ref[...]
    lo = FFN_HALO if halo else 0
    rows_all = tm + 2 * lo
    if halo:
        hbuf[0:lo] = pre(xp_ref[...])
        hbuf[lo + tm:rows_all] = pre(xn_ref[...])
    hbuf[lo:lo + tm] = pre(x)

    t_in_seq = (pl.program_id(0) * tm + lax.broadcasted_iota(jnp.int32, (tm, 1), 0)) & (seq - 1)
    is_first = t_in_seq == 0
    is_last = t_in_seq == seq - 1

    cw_chunk = D_FF // n_col_chunks
    acc = None
    for c in range(n_col_chunks):
        ca = c * cw_chunk
        a = _dot(hbuf[...], wup_ref[:, ca:ca + cw_chunk])
        gate = _dot(hbuf[lo:lo + tm], wup_ref[:, D_FF + ca:D_FF + ca + cw_chunk])
        a_prev = pltpu.roll(a, 1, 0)[lo:lo + tm]
        a_next = pltpu.roll(a, rows_all - 1, 0)[lo:lo + tm]
        a_mid = a[lo:lo + tm]
        a_prev = jnp.where(is_first, 0.0, a_prev)
        a_next = jnp.where(is_last, 0.0, a_next)
        cw = cw_ref[:, ca:ca + cw_chunk]
        conv = cw[0:1] * a_prev + cw[1:2] * a_mid + cw[2:3] * a_next + cb_ref[:, ca:ca + cw_chunk]
        act = (jax.nn.gelu(conv) * gate).astype(BF16)
        y = _dot(act, wdn_ref[ca:ca + cw_chunk, :])
        acc = y if acc is None else acc + y
    out_ref[...] = x + g2 * _rms(acc, gpost_ref[...])


def _ffn_call(x1, mod, g_pre, w_up, conv_w, conv_b, w_down, g_post, *, seq, tm, n_col_chunks):
    n = x1.shape[0]
    halo = tm % seq != 0
    row = pl.BlockSpec((tm, D_MODEL), lambda i: (i, 0))
    in_specs = [row]
    args = [x1]
    if halo:
        hb = tm // FFN_HALO
        last = n // FFN_HALO - 1
        in_specs += [pl.BlockSpec((FFN_HALO, D_MODEL), lambda i: (jnp.maximum(i * hb - 1, 0), 0)),
                     pl.BlockSpec((FFN_HALO, D_MODEL), lambda i: (jnp.minimum((i + 1) * hb, last), 0))]
        args += [x1, x1]
    in_specs += [_mod_spec(mod, tm, seq),
                 _const_spec((1, D_MODEL)), _const_spec((D_MODEL, 2 * D_FF)), _const_spec((3, D_FF)),
                 _const_spec((1, D_FF)), _const_spec((D_FF, D_MODEL)), _const_spec((1, D_MODEL))]
    args += [mod, g_pre, w_up, conv_w, conv_b, w_down, g_post]
    rows_all = tm + (2 * FFN_HALO if halo else 0)
    return pl.pallas_call(
        functools.partial(_ffn_kernel, tm=tm, seq=seq, halo=halo, n_col_chunks=n_col_chunks),
        grid=(n // tm,),
        in_specs=in_specs,
        out_specs=row,
        out_shape=jax.ShapeDtypeStruct((n, D_MODEL), F32),
        scratch_shapes=[pltpu.VMEM((rows_all, D_MODEL), BF16)],
        compiler_params=_params(1),
        name="ffn",
    )(*args)


def _rope_tables(n_tok):
    pos = jnp.arange(n_tok)
    row = (pos // GRID_W).astype(F32)
    col = (pos % GRID_W).astype(F32)
    n_freq = HEAD_DIM_B // 4
    inv = ROPE_THETA ** (-jnp.arange(n_freq, dtype=F32) / n_freq)
    ang = jnp.concatenate([row[:, None] * inv, col[:, None] * inv], axis=-1)
    cos, sin = jnp.cos(ang), jnp.sin(ang)
    cos_t = jnp.tile(cos, (1, LANES // (HEAD_DIM_B // 2)))
    sin_t = jnp.tile(jnp.concatenate([-sin, sin], axis=-1), (1, LANES // HEAD_DIM_B))
    return cos_t, sin_t


def _group(x, mod, p, lam, rope_tabs, ctx_k, ctx_v, lam_init, *, tm_proj, tm_mix, tm_ffn, tq, kc, emit_kv):
    b, t, _ = x.shape
    xf = x.reshape(b * t, D_MODEL)
    outs = _inproj_call(xf, mod, p["g_pre_mix"], p["w_in"], p["g_sgu"], rope_tabs, seq=t, tm=tm_proj,
                        emit_kv=emit_kv)
    ug, vn, q, k, v, sga, sgb = outs[:7]
    q, k, v = q.reshape(b, t, D_QK), k.reshape(b, t, D_QK), v.reshape(b, t, D_B)
    if ctx_k is None:
        o = _attn_seq_call(q, k, v, lam, p["g_subln_col"], lam_init=lam_init)
    else:
        o = _attn_call(q, k, v, ctx_k, ctx_v, lam, p["g_subln_col"], tq=tq, kc=kc, lam_init=lam_init)
    x1 = _mix_call(ug, vn, o.reshape(b * t, D_B), sga, sgb, xf, mod, p["w_s"], p["b_s_t"], p["w_a"], p["w_b"],
                   p["w_o"], p["g_post_mix"], seq=t, tm=tm_mix)
    y = _ffn_call(x1, mod, p["g_pre_ffn"], p["w_up"], p["conv_w"], p["conv_b"], p["w_down"], p["g_post_ffn"],
                  seq=t, tm=tm_ffn, n_col_chunks=2)
    kv = outs[7:] if emit_kv else None
    return y.reshape(b, t, D_MODEL), kv


def kernel(x_prompt, x_sample, c, cache_k, cache_v, c_ctx, w_ada, b_ada, g_pre_mix, g_post_mix, g_pre_ffn,
           g_post_ffn, w_in, g_sgu, w_s, b_s, lam_q1, lam_k1, lam_q2, lam_k2, g_subln, w_a, w_b, w_o, w_up,
           conv_w, conv_b, w_down):
    depth = w_in.shape[0]
    assert depth == 1, "single-layer configuration"
    l = 0
    lam_init = _lambda_init(l)
    nb, ns = x_prompt.shape[0], x_sample.shape[0]
    t_s = x_sample.shape[1]

    p = {
        "g_pre_mix": g_pre_mix[l][None], "g_post_mix": g_post_mix[l][None],
        "g_pre_ffn": g_pre_ffn[l][None], "g_post_ffn": g_post_ffn[l][None],
        "w_in": w_in[l].astype(BF16), "g_sgu": g_sgu[l][None],
        "w_s": w_s[l].astype(BF16), "b_s_t": b_s[l].T,
        "g_subln_col": g_subln[l][:, None],
        "w_a": w_a[l].astype(BF16), "w_b": w_b[l].astype(BF16), "w_o": w_o[l].astype(BF16),
        "w_up": w_up[l].astype(BF16), "conv_w": conv_w[l], "conv_b": conv_b[l][None],
        "w_down": w_down[l].astype(BF16),
    }

    cond = jnp.concatenate([c_ctx[None], c, jnp.zeros((8 - 1 - ns, D_MODEL), F32)], axis=0)
    mod, lam = _ada_call(cond, w_ada[l], b_ada[l][None], lam_q1[l][None], lam_k1[l][None], lam_q2[l][None],
                         lam_k2[l][None], lam_init)
    mod = mod.reshape(8, 6, D_MODEL)
    mod_ctx, mod_s = mod[0:1], mod[1:1 + ns]

    yp, kv = _group(x_prompt, mod_ctx, p, lam, None, None, None, lam_init,
                    tm_proj=256, tm_mix=256, tm_ffn=512, tq=256, kc=256, emit_kv=True)
    past = cache_k.shape[2]
    ctx_k = cache_k[:, l].reshape(ns, past, D_QK)
    ctx_v = cache_v[:, l].reshape(ns, past, D_B)
    ys, _ = _group(x_sample, mod_s, p, lam, _rope_tables(t_s), ctx_k, ctx_v, lam_init,
                   tm_proj=256, tm_mix=256, tm_ffn=512, tq=256, kc=512, emit_kv=False)

    seq = x_prompt.shape[1]
    new_k = kv[0].reshape(nb, 1, seq, N_HEADS_B, 2 * HEAD_DIM_B)
    new_v = kv[1].reshape(nb, 1, seq, N_HEADS_B, V_DIM_B)
    return (yp, ys, new_k, new_v)
```

```python
import functools
import math

import jax
import jax.numpy as jnp
from jax import lax
from jax.experimental import pallas as pl
from jax.experimental.pallas import tpu as pltpu

D_MODEL = 1024
GRID_W = 64
CHUNK = 128
N_GROUPS_A = 4
D_A = 512
N_HEADS_B = 8
HEAD_DIM_B = 64
V_DIM_B = 2 * HEAD_DIM_B
D_B = N_HEADS_B * V_DIM_B
D_QK = N_HEADS_B * 2 * HEAD_DIM_B
D_FF = 2816
ROPE_THETA = 10000.0
EPS = 1e-6
D_IN = 2 * D_A + 2 * D_QK + D_B + 2 * D_MODEL
OFF_U, OFF_VA, OFF_Q, OFF_K, OFF_V, OFF_GA, OFF_GB = 0, D_A, 2 * D_A, 2 * D_A + D_QK, 2 * D_A + 2 * D_QK, \
    2 * D_A + 2 * D_QK + D_B, 2 * D_A + 2 * D_QK + D_B + D_MODEL

Q_SCALE = HEAD_DIM_B ** -0.5 * math.log2(math.e)

LANES = 128
BF16_SUBLANES = 16
VMEM_LIMIT_BYTES = 56 * 1024 * 1024

BF16 = jnp.bfloat16
F32 = jnp.float32


def _lambda_init(layer_idx):
    return 0.8 - 0.6 * math.exp(-0.3 * layer_idx)


def _rms(x, g):
    r = lax.rsqrt(jnp.mean(x * x, axis=-1, keepdims=True) + EPS)
    return (x * r) * g


def _dot(a, b):
    return jnp.dot(a, b, preferred_element_type=F32)


def _const_spec(shape):
    nd = len(shape)
    return pl.BlockSpec(shape, lambda *_: (0,) * nd, pipeline_mode=pl.Buffered(1))


def _mod_spec(mod, tm, seq):
    if mod.shape[0] == 1:
        return pl.BlockSpec((None, 6, D_MODEL), lambda i: (0, 0, 0))
    tiles_per_seq = seq // tm
    return pl.BlockSpec((None, 6, D_MODEL), lambda i: (i // tiles_per_seq, 0, 0))


def _params(n_axes):
    return pltpu.CompilerParams(dimension_semantics=("arbitrary",) * n_axes,
                                vmem_limit_bytes=VMEM_LIMIT_BYTES)


def _ada_kernel(cond_ref, w_ref, b_ref, lq1_ref, lk1_ref, lq2_ref, lk2_ref, mod_ref, lam_ref, *, lam_init):
    c = cond_ref[...]
    s = (c * jax.nn.sigmoid(c)).astype(BF16)
    mod_ref[...] = _dot(s, w_ref[...].astype(BF16)) + b_ref[...]
    d1 = jnp.sum(lq1_ref[...] * lk1_ref[...], axis=-1, keepdims=True)
    d2 = jnp.sum(lq2_ref[...] * lk2_ref[...], axis=-1, keepdims=True)
    lam = jnp.exp(d1) - jnp.exp(d2) + lam_init
    lam_ref[...] = jnp.broadcast_to(lam, lam_ref.shape)


def _ada_call(cond, w_ada, b_ada, lq1, lk1, lq2, lk2, lam_init):
    rows = cond.shape[0]
    tn = D_MODEL
    vec = lambda n: pl.BlockSpec((1, n), lambda j: (0, 0))
    return pl.pallas_call(
        functools.partial(_ada_kernel, lam_init=lam_init),
        grid=(6 * D_MODEL // tn,),
        in_specs=[pl.BlockSpec((rows, D_MODEL), lambda j: (0, 0)),
                  pl.BlockSpec((D_MODEL, tn), lambda j: (0, j)),
                  pl.BlockSpec((1, tn), lambda j: (0, j)),
                  vec(HEAD_DIM_B), vec(HEAD_DIM_B), vec(HEAD_DIM_B), vec(HEAD_DIM_B)],
        out_specs=[pl.BlockSpec((rows, tn), lambda j: (0, j)),
                   pl.BlockSpec((8, LANES), lambda j: (0, 0))],
        out_shape=[jax.ShapeDtypeStruct((rows, 6 * D_MODEL), F32),
                   jax.ShapeDtypeStruct((8, LANES), F32)],
        compiler_params=_params(1),
        name="ada",
    )(cond, w_ada, b_ada, lq1, lk1, lq2, lk2)


def _inproj_kernel(*refs, tm, rope, emit_kv):
    refs = list(refs)
    x_ref, mod_ref, gpre_ref, w_ref, gsgu_ref, ws_ref, bs_ref, wa_ref = refs[:8]
    pos = 8
    if rope:
        cos_ref, sin_ref = refs[pos:pos + 2]
        pos += 2
    gta_ref, q_ref, k_ref, v_ref, sgb_ref = refs[pos:pos + 5]
    pos += 5
    if emit_kv:
        kf_ref, vf_ref = refs[pos:pos + 2]
        pos += 2
    h_scr, a_scr = refs[pos:pos + 2]

    mod = mod_ref[...]
    sh1, sc1 = mod[0:1], mod[1:2]
    h_scr[...] = (_rms(x_ref[...], gpre_ref[...]) * (1.0 + sc1) + sh1).astype(BF16)

    def proj(off, width):
        return _dot(h_scr[...], w_ref[:, off:off + width])

    ug = jax.nn.gelu(proj(OFF_U, D_A))
    vn = _rms(jax.nn.gelu(proj(OFF_VA, D_A)), gsgu_ref[...]).astype(BF16)
    for j in range(tm // CHUNK):
        rows = slice(j * CHUNK, (j + 1) * CHUNK)
        for g in range(N_GROUPS_A):
            cols = slice(g * LANES, (g + 1) * LANES)
            mixed = _dot(ws_ref[g], vn[rows, cols]) + bs_ref[:, g:g + 1]
            a_scr[rows, cols] = (ug[rows, cols] * mixed).astype(BF16)
    gta_ref[...] = jax.nn.sigmoid(proj(OFF_GA, D_MODEL)) * _dot(a_scr[...], wa_ref[...])

    if rope:
        cos_t, sin_t = cos_ref[...], sin_ref[...]
        lane = lax.broadcasted_iota(jnp.int32, (tm, LANES), 1)
        first_half = (lane & (HEAD_DIM_B // 2)) == 0

    def rotary(xh):
        partner = jnp.where(first_half, pltpu.roll(xh, LANES - HEAD_DIM_B // 2, 1),
                            pltpu.roll(xh, HEAD_DIM_B // 2, 1))
        return xh * cos_t + partner * sin_t

    q = proj(OFF_Q, D_QK)
    for hb in range(N_HEADS_B):
        cols = slice(hb * LANES, (hb + 1) * LANES)
        qh = q[:, cols]
        if rope:
            qh = rotary(qh)
        q_ref[:, cols] = (qh * Q_SCALE).astype(BF16)

    k = proj(OFF_K, D_QK)
    if emit_kv:
        kf_ref[...] = k
    for hb in range(N_HEADS_B):
        cols = slice(hb * LANES, (hb + 1) * LANES)
        kh = k[:, cols]
        if rope:
            kh = rotary(kh)
        k_ref[:, cols] = kh.astype(BF16)

    v = proj(OFF_V, D_B)
    if emit_kv:
        vf_ref[...] = v
    v_ref[...] = v.astype(BF16)

    sgb_ref[...] = jax.nn.sigmoid(proj(OFF_GB, D_MODEL))


def _inproj_call(x, mod, g_pre, w_in, g_sgu, w_s, b_s_t, w_a, rope_tabs, *, seq, tm, emit_kv):
    n = x.shape[0]
    tiles_per_seq = seq // tm
    rope = rope_tabs is not None
    row = lambda w: pl.BlockSpec((tm, w), lambda i: (i, 0))
    in_specs = [row(D_MODEL), _mod_spec(mod, tm, seq),
                _const_spec((1, D_MODEL)), _const_spec((D_MODEL, D_IN)), _const_spec((1, D_A)),
                _const_spec((N_GROUPS_A, CHUNK, CHUNK)), _const_spec((CHUNK, N_GROUPS_A)),
                _const_spec((D_A, D_MODEL))]
    args = [x, mod, g_pre, w_in, g_sgu, w_s, b_s_t, w_a]
    if rope:
        tab = pl.BlockSpec((tm, LANES), lambda i: (i % tiles_per_seq, 0))
        in_specs += [tab, tab]
        args += list(rope_tabs)
    widths = [(D_MODEL, F32), (D_QK, BF16), (D_QK, BF16), (D_B, BF16), (D_MODEL, F32)]
    if emit_kv:
        widths += [(D_QK, F32), (D_B, F32)]
    return pl.pallas_call(
        functools.partial(_inproj_kernel, tm=tm, rope=rope, emit_kv=emit_kv),
        grid=(n // tm,),
        in_specs=in_specs,
        out_specs=[row(w) for w, _ in widths],
        out_shape=[jax.ShapeDtypeStruct((n, w), dt) for w, dt in widths],
        scratch_shapes=[pltpu.VMEM((tm, D_MODEL), BF16), pltpu.VMEM((tm, D_A), BF16)],
        compiler_params=_params(1),
        name="inproj",
    )(*args)


def _two_map_queries(q):
    lane = lax.broadcasted_iota(jnp.int32, q.shape, 1)
    zero = jnp.zeros_like(q)
    return jnp.concatenate([jnp.where(lane < HEAD_DIM_B, q, zero), jnp.where(lane >= HEAD_DIM_B, q, zero)], axis=0)


def _scores_t(k, qz):
    return lax.dot_general(k, qz, (((1,), (1,)), ((), ())), preferred_element_type=F32)


def _diff_head_out(acc, l, lam, g_col, tq, lam_init):
    rl = 1.0 / l
    o = acc[:, :tq] * rl[:, :tq] - lam * (acc[:, tq:] * rl[:, tq:])
    r = lax.rsqrt(jnp.mean(o * o, axis=0, keepdims=True) + EPS)
    y = (o * r) * g_col * (1.0 - lam_init)
    return y.T.astype(BF16)


def _attn_kernel(qa_ref, qb_ref, k_ref, v_ref, ck_ref, cv_ref, lam_ref, gsub_ref, o_ref, kall, vt, s_even, s_odd,
                 mbuf, *, tq, kc, n_ctx, n_own, n_q, n_tiles, tiles_per_step, lam_init):
    p_tiles = tiles_per_step
    n_chunks = n_ctx + n_own
    i = pl.program_id(0)
    newest = p_tiles * i

    @pl.when(jnp.logical_and(newest % n_q == 0, newest < n_tiles))
    def _stage_keys():
        slot = (newest // n_q) % 2
        head = (newest // n_q) % N_HEADS_B
        for c in range(n_ctx):
            rows = pl.ds(c * kc * N_HEADS_B + head, kc, stride=N_HEADS_B)
            kall[slot, c] = ck_ref[rows, :].astype(BF16)
            vt[slot, c] = cv_ref[rows, :].T.astype(BF16)
        for c in range(n_own):
            rows = slice(c * kc, (c + 1) * kc)
            kall[slot, n_ctx + c] = k_ref[rows, :]
            vt[slot, n_ctx + c] = v_ref[rows, :].astype(F32).T.astype(BF16)

    @pl.when(i == 0)
    def _define_first_read():
        s_even[...] = jnp.zeros(s_even.shape, F32)
        mbuf[0] = jnp.zeros(mbuf.shape[1:], F32)

    s_bufs = (s_even, s_odd)
    lam = lam_ref[0:1, 0:1]
    g_col = gsub_ref[...]
    first = p_tiles * (i - 1)
    for u in range(p_tiles):
        k_slot = (jnp.clip(first + u + 1, 0, n_tiles - 1) // n_q) % 2
        v_slot = (jnp.clip(first + u, 0, n_tiles - 1) // n_q) % 2
        q = qa_ref[(u + 1) * tq:(u + 2) * tq, :] if u < p_tiles - 1 else qb_ref[...]
        qz = _two_map_queries(q)
        s_cur, s_prev = s_bufs[(u + 1) % 2], s_bufs[u % 2]
        m_prev = mbuf[u % 2]
        m = jnp.full((1, 2 * tq), -jnp.inf, F32)
        l = jnp.zeros((1, 2 * tq), F32)
        acc = jnp.zeros((V_DIM_B, 2 * tq), F32)
        for c in range(n_chunks):
            s = _scores_t(kall[k_slot, c], qz)
            s_cur[c] = s
            m = jnp.maximum(m, jnp.max(s, axis=0, keepdims=True))
            p = jnp.exp2(s_prev[c] - m_prev)
            l = l + jnp.sum(p, axis=0, keepdims=True)
            acc = acc + _dot(vt[v_slot, c], p.astype(BF16))
        mbuf[(u + 1) % 2] = m
        o_ref[u * tq:(u + 1) * tq, :] = _diff_head_out(acc, l, lam, g_col, tq, lam_init)


def _attn_call(q, k, v, ctx_k, ctx_v, lam, g_sub_col, *, tq, kc, tiles_per_step, lam_init):
    b, t, _ = q.shape
    n_q = t // tq
    n_tiles = b * N_HEADS_B * n_q
    p_tiles = tiles_per_step
    assert p_tiles % 2 == 0 and n_q % p_tiles == 0
    n_own = t // kc
    ctx_rows = ctx_k.shape[1]
    n_ctx = ctx_rows // N_HEADS_B // kc
    n_chunks = n_own + n_ctx

    def head_of(tile):
        g = tile // n_q
        return g // N_HEADS_B, g % N_HEADS_B

    def group_map(i):
        tile = p_tiles * jnp.maximum(i - 1, 0)
        bi, h = head_of(tile)
        return bi, (tile % n_q) // p_tiles, h

    def newest_map(i):
        tile = jnp.minimum(p_tiles * i, n_tiles - 1)
        bi, h = head_of(tile)
        return bi, tile % n_q, h

    def head_map(i):
        bi, h = head_of(jnp.minimum(p_tiles * i, n_tiles - 1))
        return bi, 0, h

    def batch_map(i):
        return head_map(i)[0], 0, 0

    head_rows = lambda rows: pl.BlockSpec((None, rows, LANES), head_map)
    ctx_spec = pl.BlockSpec((None, ctx_rows, LANES), batch_map)
    group_spec = pl.BlockSpec((None, p_tiles * tq, LANES), group_map)
    return pl.pallas_call(
        functools.partial(_attn_kernel, tq=tq, kc=kc, n_ctx=n_ctx, n_own=n_own, n_q=n_q, n_tiles=n_tiles,
                          tiles_per_step=p_tiles, lam_init=lam_init),
        grid=(n_tiles // p_tiles + 1,),
        in_specs=[group_spec, pl.BlockSpec((None, tq, LANES), newest_map),
                  head_rows(t), head_rows(t), ctx_spec, ctx_spec,
                  pl.BlockSpec((8, LANES), lambda i: (0, 0)),
                  pl.BlockSpec((V_DIM_B, 1), lambda i: (0, 0))],
        out_specs=group_spec,
        out_shape=jax.ShapeDtypeStruct((b, t, D_B), BF16),
        scratch_shapes=[pltpu.VMEM((2, n_chunks, kc, LANES), BF16),
                        pltpu.VMEM((2, n_chunks, V_DIM_B, kc), BF16),
                        pltpu.VMEM((n_chunks, kc, 2 * tq), F32),
                        pltpu.VMEM((n_chunks, kc, 2 * tq), F32),
                        pltpu.VMEM((2, 1, 2 * tq), F32)],
        compiler_params=_params(1),
        name="attn",
    )(q, q, k, v, ctx_k, ctx_v, lam, g_sub_col)


def _attn_seq_kernel(q_ref, k_ref, v_ref, lam_ref, gsub_ref, o_ref, *, t, lam_init):
    lam = lam_ref[0:1, 0:1]
    g_col = gsub_ref[...]
    for h in range(N_HEADS_B):
        cols = slice(h * LANES, (h + 1) * LANES)
        s = _scores_t(k_ref[:, cols], _two_map_queries(q_ref[:, cols]))
        p = jnp.exp2(s - jnp.max(s, axis=0, keepdims=True))
        l = jnp.sum(p, axis=0, keepdims=True)
        v_t = v_ref[:, cols].astype(F32).T.astype(BF16)
        acc = _dot(v_t, p.astype(BF16))
        o_ref[:, cols] = _diff_head_out(acc, l, lam, g_col, t, lam_init)


def _attn_seq_call(q, k, v, lam, g_sub_col, *, lam_init):
    b, t, _ = q.shape
    seq_block = pl.BlockSpec((None, t, D_B), lambda bi: (bi, 0, 0))
    return pl.pallas_call(
        functools.partial(_attn_seq_kernel, t=t, lam_init=lam_init),
        grid=(b,),
        in_specs=[seq_block, seq_block, seq_block,
                  pl.BlockSpec((8, LANES), lambda bi: (0, 0)),
                  pl.BlockSpec((V_DIM_B, 1), lambda bi: (0, 0))],
        out_specs=seq_block,
        out_shape=jax.ShapeDtypeStruct((b, t, D_B), BF16),
        compiler_params=_params(1),
        name="attn_seq",
    )(q, k, v, lam, g_sub_col)


def _mix_kernel(gta_ref, o_ref, sgb_ref, x_ref, mod_ref, wb_ref, wo_ref, gpost_ref, x1_ref):
    merged = gta_ref[...] + sgb_ref[...] * _dot(o_ref[...], wb_ref[...])
    mix = _dot(merged.astype(BF16), wo_ref[...])
    g1 = mod_ref[2:3, :]
    x1_ref[...] = x_ref[...] + g1 * _rms(mix, gpost_ref[...])


def _mix_call(gta, o, sgb, x, mod, w_b, w_o, g_post, *, seq, tm):
    n = x.shape[0]
    row = lambda w: pl.BlockSpec((tm, w), lambda i: (i, 0))
    return pl.pallas_call(
        _mix_kernel,
        grid=(n // tm,),
        in_specs=[row(D_MODEL), row(D_B), row(D_MODEL), row(D_MODEL), _mod_spec(mod, tm, seq),
                  _const_spec((D_B, D_MODEL)), _const_spec((D_MODEL, D_MODEL)), _const_spec((1, D_MODEL))],
        out_specs=row(D_MODEL),
        out_shape=jax.ShapeDtypeStruct((n, D_MODEL), F32),
        compiler_params=_params(1),
        name="mix",
    )(gta, o, sgb, x, mod, w_b, w_o, g_post)


FFN_HALO = BF16_SUBLANES


def _ffn_kernel(*refs, tm, seq, halo, n_col_chunks):
    refs = list(refs)
    x_ref = refs[0]
    pos = 1
    if halo:
        xp_ref, xn_ref = refs[1:3]
        pos = 3
    mod_ref, gpre_ref, wup_ref, cw_ref, cb_ref, wdn_ref, gpost_ref, out_ref, hbuf = refs[pos:pos + 9]

    mod = mod_ref[...]
    sh2, sc2, g2 = mod[3:4], mod[4:5], mod[5:6]

    def pre(xv):
        return (_rms(xv, gpre_ref[...]) * (1.0 + sc2) + sh2).astype(BF16)

    x = x_ref[...]
    lo = FFN_HALO if halo else 0
    rows_all = tm + 2 * lo
    if halo:
        hbuf[0:lo] = pre(xp_ref[...])
        hbuf[lo + tm:rows_all] = pre(xn_ref[...])
    hbuf[lo:lo + tm] = pre(x)

    t_in_seq = (pl.program_id(0) * tm + lax.broadcasted_iota(jnp.int32, (tm, 1), 0)) & (seq - 1)
    is_first = t_in_seq == 0
    is_last = t_in_seq == seq - 1

    cw_chunk = D_FF // n_col_chunks
    acc = None
    for c in range(n_col_chunks):
        ca = c * cw_chunk
        a = _dot(hbuf[...], wup_ref[:, ca:ca + cw_chunk])
        gate = _dot(hbuf[lo:lo + tm], wup_ref[:, D_FF + ca:D_FF + ca + cw_chunk])
        a_prev = pltpu.roll(a, 1, 0)[lo:lo + tm]
        a_next = pltpu.roll(a, rows_all - 1, 0)[lo:lo + tm]
        a_mid = a[lo:lo + tm]
        a_prev = jnp.where(is_first, 0.0, a_prev)
        a_next = jnp.where(is_last, 0.0, a_next)
        cw = cw_ref[:, ca:ca + cw_chunk]
        conv = cw[0:1] * a_prev + cw[1:2] * a_mid + cw[2:3] * a_next + cb_ref[:, ca:ca + cw_chunk]
        act = (jax.nn.gelu(conv) * gate).astype(BF16)
        y = _dot(act, wdn_ref[ca:ca + cw_chunk, :])
        acc = y if acc is None else acc + y
    out_ref[...] = x + g2 * _rms(acc, gpost_ref[...])


def _ffn_call(x1, mod, g_pre, w_up, conv_w, conv_b, w_down, g_post, *, seq, tm, n_col_chunks):
    n = x1.shape[0]
    halo = tm % seq != 0
    row = pl.BlockSpec((tm, D_MODEL), lambda i: (i, 0))
    in_specs = [row]
    args = [x1]
    if halo:
        hb = tm // FFN_HALO
        last = n // FFN_HALO - 1
        in_specs += [pl.BlockSpec((FFN_HALO, D_MODEL), lambda i: (jnp.maximum(i * hb - 1, 0), 0)),
                     pl.BlockSpec((FFN_HALO, D_MODEL), lambda i: (jnp.minimum((i + 1) * hb, last), 0))]
        args += [x1, x1]
    in_specs += [_mod_spec(mod, tm, seq),
                 _const_spec((1, D_MODEL)), _const_spec((D_MODEL, 2 * D_FF)), _const_spec((3, D_FF)),
                 _const_spec((1, D_FF)), _const_spec((D_FF, D_MODEL)), _const_spec((1, D_MODEL))]
    args += [mod, g_pre, w_up, conv_w, conv_b, w_down, g_post]
    rows_all = tm + (2 * FFN_HALO if halo else 0)
    return pl.pallas_call(
        functools.partial(_ffn_kernel, tm=tm, seq=seq, halo=halo, n_col_chunks=n_col_chunks),
        grid=(n // tm,),
        in_specs=in_specs,
        out_specs=row,
        out_shape=jax.ShapeDtypeStruct((n, D_MODEL), F32),
        scratch_shapes=[pltpu.VMEM((rows_all, D_MODEL), BF16)],
        compiler_params=_params(1),
        name="ffn",
    )(*args)


def _rope_tables(n_tok):
    pos = jnp.arange(n_tok)
    row = (pos // GRID_W).astype(F32)
    col = (pos % GRID_W).astype(F32)
    n_freq = HEAD_DIM_B // 4
    inv = ROPE_THETA ** (-jnp.arange(n_freq, dtype=F32) / n_freq)
    ang = jnp.concatenate([row[:, None] * inv, col[:, None] * inv], axis=-1)
    cos, sin = jnp.cos(ang), jnp.sin(ang)
    cos_t = jnp.tile(cos, (1, LANES // (HEAD_DIM_B // 2)))
    sin_t = jnp.tile(jnp.concatenate([-sin, sin], axis=-1), (1, LANES // HEAD_DIM_B))
    return cos_t, sin_t


def _group(x, mod, p, lam, rope_tabs, ctx_k, ctx_v, lam_init, *, tm_proj, tm_mix, tm_ffn, tq, kc, emit_kv):
    b, t, _ = x.shape
    xf = x.reshape(b * t, D_MODEL)
    outs = _inproj_call(xf, mod, p["g_pre_mix"], p["w_in"], p["g_sgu"], p["w_s"], p["b_s_t"], p["w_a"], rope_tabs,
                        seq=t, tm=tm_proj, emit_kv=emit_kv)
    gta, q, k, v, sgb = outs[:5]
    q, k, v = q.reshape(b, t, D_QK), k.reshape(b, t, D_QK), v.reshape(b, t, D_B)
    if ctx_k is None:
        o = _attn_seq_call(q, k, v, lam, p["g_subln_col"], lam_init=lam_init)
    else:
        o = _attn_call(q, k, v, ctx_k, ctx_v, lam, p["g_subln_col"], tq=tq, kc=kc, tiles_per_step=2,
                       lam_init=lam_init)
    x1 = _mix_call(gta, o.reshape(b * t, D_B), sgb, xf, mod, p["w_b"], p["w_o"], p["g_post_mix"], seq=t, tm=tm_mix)
    y = _ffn_call(x1, mod, p["g_pre_ffn"], p["w_up"], p["conv_w"], p["conv_b"], p["w_down"], p["g_post_ffn"],
                  seq=t, tm=tm_ffn, n_col_chunks=2)
    kv = outs[5:] if emit_kv else None
    return y.reshape(b, t, D_MODEL), kv


def kernel(x_prompt, x_sample, c, cache_k, cache_v, c_ctx, w_ada, b_ada, g_pre_mix, g_post_mix, g_pre_ffn,
           g_post_ffn, w_in, g_sgu, w_s, b_s, lam_q1, lam_k1, lam_q2, lam_k2, g_subln, w_a, w_b, w_o, w_up,
           conv_w, conv_b, w_down):
    depth = w_in.shape[0]
    assert depth == 1, "single-layer configuration"
    l = 0
    lam_init = _lambda_init(l)
    nb, ns = x_prompt.shape[0], x_sample.shape[0]
    t_s = x_sample.shape[1]

    p = {
        "g_pre_mix": g_pre_mix[l][None], "g_post_mix": g_post_mix[l][None],
        "g_pre_ffn": g_pre_ffn[l][None], "g_post_ffn": g_post_ffn[l][None],
        "w_in": w_in[l].astype(BF16), "g_sgu": g_sgu[l][None],
        "w_s": w_s[l].astype(BF16), "b_s_t": b_s[l].T,
        "g_subln_col": g_subln[l][:, None],
        "w_a": w_a[l].astype(BF16), "w_b": w_b[l].astype(BF16), "w_o": w_o[l].astype(BF16),
        "w_up": w_up[l].astype(BF16), "conv_w": conv_w[l], "conv_b": conv_b[l][None],
        "w_down": w_down[l].astype(BF16),
    }

    cond = jnp.concatenate([c_ctx[None], c, jnp.zeros((8 - 1 - ns, D_MODEL), F32)], axis=0)
    mod, lam = _ada_call(cond, w_ada[l], b_ada[l][None], lam_q1[l][None], lam_k1[l][None], lam_q2[l][None],
                         lam_k2[l][None], lam_init)
    mod = mod.reshape(8, 6, D_MODEL)
    mod_ctx, mod_s = mod[0:1], mod[1:1 + ns]

    yp, kv = _group(x_prompt, mod_ctx, p, lam, None, None, None, lam_init,
                    tm_proj=256, tm_mix=256, tm_ffn=512, tq=256, kc=256, emit_kv=True)
    past = cache_k.shape[2]
    ctx_k = cache_k.reshape(ns, past * N_HEADS_B, 2 * HEAD_DIM_B)
    ctx_v = cache_v.reshape(ns, past * N_HEADS_B, V_DIM_B)
    ys, _ = _group(x_sample, mod_s, p, lam, _rope_tables(t_s), ctx_k, ctx_v, lam_init,
                   tm_proj=256, tm_mix=256, tm_ffn=512, tq=256, kc=512, emit_kv=False)

    seq = x_prompt.shape[1]
    new_k = kv[0].reshape(nb, 1, seq, N_HEADS_B, 2 * HEAD_DIM_B)
    new_v = kv[1].reshape(nb, 1, seq, N_HEADS_B, V_DIM_B)
    return (yp, ys, new_k, new_v)
```

```python
import functools
import math

import jax
import jax.numpy as jnp
from jax import lax
from jax.experimental import pallas as pl
from jax.experimental.pallas import tpu as pltpu

D_MODEL = 1024
GRID_W = 64
CHUNK = 128
N_GROUPS_A = 4
D_A = 512
N_HEADS_B = 8
HEAD_DIM_B = 64
V_DIM_B = 2 * HEAD_DIM_B
D_B = N_HEADS_B * V_DIM_B
D_QK = N_HEADS_B * 2 * HEAD_DIM_B
D_FF = 2816
ROPE_THETA = 10000.0
EPS = 1e-6
D_IN = 2 * D_A + 2 * D_QK + D_B + 2 * D_MODEL
OFF_U, OFF_VA, OFF_Q, OFF_K, OFF_V, OFF_GA, OFF_GB = 0, D_A, 2 * D_A, 2 * D_A + D_QK, 2 * D_A + 2 * D_QK, \
    2 * D_A + 2 * D_QK + D_B, 2 * D_A + 2 * D_QK + D_B + D_MODEL

Q_SCALE = HEAD_DIM_B ** -0.5 * math.log2(math.e)

LANES = 128
BF16_SUBLANES = 16
VMEM_LIMIT_BYTES = 56 * 1024 * 1024

BF16 = jnp.bfloat16
F32 = jnp.float32


def _lambda_init(layer_idx):
    return 0.8 - 0.6 * math.exp(-0.3 * layer_idx)


def _rms(x, g):
    r = lax.rsqrt(jnp.mean(x * x, axis=-1, keepdims=True) + EPS)
    return (x * r) * g


def _dot(a, b):
    return jnp.dot(a, b, preferred_element_type=F32)


def _const_spec(shape):
    nd = len(shape)
    return pl.BlockSpec(shape, lambda *_: (0,) * nd, pipeline_mode=pl.Buffered(1))


def _mod_spec(mod, tm, seq):
    if mod.shape[0] == 1:
        return pl.BlockSpec((None, 6, D_MODEL), lambda i: (0, 0, 0))
    tiles_per_seq = seq // tm
    return pl.BlockSpec((None, 6, D_MODEL), lambda i: (i // tiles_per_seq, 0, 0))


def _params(n_axes):
    return pltpu.CompilerParams(dimension_semantics=("arbitrary",) * n_axes,
                                vmem_limit_bytes=VMEM_LIMIT_BYTES)


def _ada_kernel(cond_ref, w_ref, b_ref, lq1_ref, lk1_ref, lq2_ref, lk2_ref, mod_ref, lam_ref, *, lam_init):
    c = cond_ref[...]
    s = (c * jax.nn.sigmoid(c)).astype(BF16)
    mod_ref[...] = _dot(s, w_ref[...].astype(BF16)) + b_ref[...]
    d1 = jnp.sum(lq1_ref[...] * lk1_ref[...], axis=-1, keepdims=True)
    d2 = jnp.sum(lq2_ref[...] * lk2_ref[...], axis=-1, keepdims=True)
    lam = jnp.exp(d1) - jnp.exp(d2) + lam_init
    lam_ref[...] = jnp.broadcast_to(lam, lam_ref.shape)


def _ada_call(cond, w_ada, b_ada, lq1, lk1, lq2, lk2, lam_init):
    rows = cond.shape[0]
    tn = D_MODEL
    vec = lambda n: pl.BlockSpec((1, n), lambda j: (0, 0))
    return pl.pallas_call(
        functools.partial(_ada_kernel, lam_init=lam_init),
        grid=(6 * D_MODEL // tn,),
        in_specs=[pl.BlockSpec((rows, D_MODEL), lambda j: (0, 0)),
                  pl.BlockSpec((D_MODEL, tn), lambda j: (0, j)),
                  pl.BlockSpec((1, tn), lambda j: (0, j)),
                  vec(HEAD_DIM_B), vec(HEAD_DIM_B), vec(HEAD_DIM_B), vec(HEAD_DIM_B)],
        out_specs=[pl.BlockSpec((rows, tn), lambda j: (0, j)),
                   pl.BlockSpec((8, LANES), lambda j: (0, 0))],
        out_shape=[jax.ShapeDtypeStruct((rows, 6 * D_MODEL), F32),
                   jax.ShapeDtypeStruct((8, LANES), F32)],
        compiler_params=_params(1),
        name="ada",
    )(cond, w_ada, b_ada, lq1, lk1, lq2, lk2)


def _inproj_kernel(*refs, tm, rope, emit_kv):
    refs = list(refs)
    x_ref, mod_ref, gpre_ref, w_ref, gsgu_ref, ws_ref, bs_ref, wa_ref = refs[:8]
    pos = 8
    if rope:
        cos_ref, sin_ref = refs[pos:pos + 2]
        pos += 2
    gta_ref, q_ref, k_ref, v_ref, sgb_ref = refs[pos:pos + 5]
    pos += 5
    if emit_kv:
        kf_ref, vf_ref = refs[pos:pos + 2]
        pos += 2
    h_scr, a_scr = refs[pos:pos + 2]

    mod = mod_ref[...]
    sh1, sc1 = mod[0:1], mod[1:2]
    h_scr[...] = (_rms(x_ref[...], gpre_ref[...]) * (1.0 + sc1) + sh1).astype(BF16)

    def proj(off, width):
        return _dot(h_scr[...], w_ref[:, off:off + width])

    ug = jax.nn.gelu(proj(OFF_U, D_A))
    vn = _rms(jax.nn.gelu(proj(OFF_VA, D_A)), gsgu_ref[...]).astype(BF16)
    for j in range(tm // CHUNK):
        rows = slice(j * CHUNK, (j + 1) * CHUNK)
        for g in range(N_GROUPS_A):
            cols = slice(g * LANES, (g + 1) * LANES)
            mixed = _dot(ws_ref[g], vn[rows, cols]) + bs_ref[:, g:g + 1]
            a_scr[rows, cols] = (ug[rows, cols] * mixed).astype(BF16)
    gta_ref[...] = jax.nn.sigmoid(proj(OFF_GA, D_MODEL)) * _dot(a_scr[...], wa_ref[...])

    if rope:
        cos_t, sin_t = cos_ref[...], sin_ref[...]
        lane = lax.broadcasted_iota(jnp.int32, (tm, LANES), 1)
        first_half = (lane & (HEAD_DIM_B // 2)) == 0

    def rotary(xh):
        partner = jnp.where(first_half, pltpu.roll(xh, LANES - HEAD_DIM_B // 2, 1),
                            pltpu.roll(xh, HEAD_DIM_B // 2, 1))
        return xh * cos_t + partner * sin_t

    q = proj(OFF_Q, D_QK)
    for hb in range(N_HEADS_B):
        cols = slice(hb * LANES, (hb + 1) * LANES)
        qh = q[:, cols]
        if rope:
            qh = rotary(qh)
        q_ref[:, cols] = (qh * Q_SCALE).astype(BF16)

    k = proj(OFF_K, D_QK)
    if emit_kv:
        kf_ref[...] = k
    for hb in range(N_HEADS_B):
        cols = slice(hb * LANES, (hb + 1) * LANES)
        kh = k[:, cols]
        if rope:
            kh = rotary(kh)
        k_ref[:, cols] = kh.astype(BF16)

    v = proj(OFF_V, D_B)
    if emit_kv:
        vf_ref[...] = v
    v_ref[...] = v.astype(BF16)

    sgb_ref[...] = jax.nn.sigmoid(proj(OFF_GB, D_MODEL))


def _inproj_call(x, mod, g_pre, w_in, g_sgu, w_s, b_s_t, w_a, rope_tabs, *, seq, tm, emit_kv):
    n = x.shape[0]
    tiles_per_seq = seq // tm
    rope = rope_tabs is not None
    row = lambda w: pl.BlockSpec((tm, w), lambda i: (i, 0))
    in_specs = [row(D_MODEL), _mod_spec(mod, tm, seq),
                _const_spec((1, D_MODEL)), _const_spec((D_MODEL, D_IN)), _const_spec((1, D_A)),
                _const_spec((N_GROUPS_A, CHUNK, CHUNK)), _const_spec((CHUNK, N_GROUPS_A)),
                _const_spec((D_A, D_MODEL))]
    args = [x, mod, g_pre, w_in, g_sgu, w_s, b_s_t, w_a]
    if rope:
        tab = pl.BlockSpec((tm, LANES), lambda i: (i % tiles_per_seq, 0))
        in_specs += [tab, tab]
        args += list(rope_tabs)
    widths = [(D_MODEL, F32), (D_QK, BF16), (D_QK, BF16), (D_B, BF16), (D_MODEL, F32)]
    if emit_kv:
        widths += [(D_QK, F32), (D_B, F32)]
    return pl.pallas_call(
        functools.partial(_inproj_kernel, tm=tm, rope=rope, emit_kv=emit_kv),
        grid=(n // tm,),
        in_specs=in_specs,
        out_specs=[row(w) for w, _ in widths],
        out_shape=[jax.ShapeDtypeStruct((n, w), dt) for w, dt in widths],
        scratch_shapes=[pltpu.VMEM((tm, D_MODEL), BF16), pltpu.VMEM((tm, D_A), BF16)],
        compiler_params=_params(1),
        name="inproj",
    )(*args)


def _two_map_queries(q):
    lane = lax.broadcasted_iota(jnp.int32, q.shape, 1)
    zero = jnp.zeros_like(q)
    return jnp.concatenate([jnp.where(lane < HEAD_DIM_B, q, zero), jnp.where(lane >= HEAD_DIM_B, q, zero)], axis=0)


def _scores_t(k, qz):
    return lax.dot_general(k, qz, (((1,), (1,)), ((), ())), preferred_element_type=F32)


def _diff_head_out(acc, l, lam, g_col, tq, lam_init):
    rl = 1.0 / l
    o = acc[:, :tq] * rl[:, :tq] - lam * (acc[:, tq:] * rl[:, tq:])
    r = lax.rsqrt(jnp.mean(o * o, axis=0, keepdims=True) + EPS)
    y = (o * r) * g_col * (1.0 - lam_init)
    return y.T.astype(BF16)


def _attn_kernel(qa_ref, qb_ref, k_ref, v_ref, ck_ref, cv_ref, lam_ref, gsub_ref, o_ref, kall, vt, s_even, s_odd,
                 mbuf, *, tq, kc, n_ctx, n_own, n_q, n_tiles, tiles_per_step, lam_init):
    p_tiles = tiles_per_step
    n_chunks = n_ctx + n_own
    i = pl.program_id(0)
    newest = p_tiles * i

    @pl.when(jnp.logical_and(newest % n_q == 0, newest < n_tiles))
    def _stage_keys():
        slot = (newest // n_q) % 2
        head = (newest // n_q) % N_HEADS_B
        for c in range(n_ctx):
            rows = pl.ds(c * kc * N_HEADS_B + head, kc, stride=N_HEADS_B)
            kall[slot, c] = ck_ref[rows, :].astype(BF16)
            vt[slot, c] = cv_ref[rows, :].T.astype(BF16)
        for c in range(n_own):
            rows = slice(c * kc, (c + 1) * kc)
            kall[slot, n_ctx + c] = k_ref[rows, :]
            vt[slot, n_ctx + c] = v_ref[rows, :].astype(F32).T.astype(BF16)

    @pl.when(i == 0)
    def _define_first_read():
        s_even[...] = jnp.zeros(s_even.shape, F32)
        mbuf[0] = jnp.zeros(mbuf.shape[1:], F32)

    s_bufs = (s_even, s_odd)
    lam = lam_ref[0:1, 0:1]
    g_col = gsub_ref[...]
    first = p_tiles * (i - 1)
    for u in range(p_tiles):
        k_slot = (jnp.clip(first + u + 1, 0, n_tiles - 1) // n_q) % 2
        v_slot = (jnp.clip(first + u, 0, n_tiles - 1) // n_q) % 2
        q = qa_ref[(u + 1) * tq:(u + 2) * tq, :] if u < p_tiles - 1 else qb_ref[...]
        qz = _two_map_queries(q)
        s_cur, s_prev = s_bufs[(u + 1) % 2], s_bufs[u % 2]
        m_prev = mbuf[u % 2]
        m = jnp.full((1, 2 * tq), -jnp.inf, F32)
        l = jnp.zeros((1, 2 * tq), F32)
        acc = jnp.zeros((V_DIM_B, 2 * tq), F32)
        for c in range(n_chunks):
            s = _scores_t(kall[k_slot, c], qz)
            s_cur[c] = s
            m = jnp.maximum(m, jnp.max(s, axis=0, keepdims=True))
            p = jnp.exp2(s_prev[c] - m_prev)
            l = l + jnp.sum(p, axis=0, keepdims=True)
            acc = acc + _dot(vt[v_slot, c], p.astype(BF16))
        mbuf[(u + 1) % 2] = m
        o_ref[u * tq:(u + 1) * tq, :] = _diff_head_out(acc, l, lam, g_col, tq, lam_init)


def _attn_call(q, k, v, ctx_k, ctx_v, lam, g_sub_col, *, tq, kc, tiles_per_step, lam_init):
    b, t, _ = q.shape
    n_q = t // tq
    n_tiles = b * N_HEADS_B * n_q
    p_tiles = tiles_per_step
    assert p_tiles % 2 == 0 and n_q % p_tiles == 0
    n_own = t // kc
    ctx_rows = ctx_k.shape[1]
    n_ctx = ctx_rows // N_HEADS_B // kc
    n_chunks = n_own + n_ctx

    def head_of(tile):
        g = tile // n_q
        return g // N_HEADS_B, g % N_HEADS_B

    def group_map(i):
        tile = p_tiles * jnp.maximum(i - 1, 0)
        bi, h = head_of(tile)
        return bi, (tile % n_q) // p_tiles, h

    def newest_map(i):
        tile = jnp.minimum(p_tiles * i, n_tiles - 1)
        bi, h = head_of(tile)
        return bi, tile % n_q, h

    def head_map(i):
        bi, h = head_of(jnp.minimum(p_tiles * i, n_tiles - 1))
        return bi, 0, h

    def batch_map(i):
        return head_map(i)[0], 0, 0

    head_rows = lambda rows: pl.BlockSpec((None, rows, LANES), head_map)
    ctx_spec = pl.BlockSpec((None, ctx_rows, LANES), batch_map)
    group_spec = pl.BlockSpec((None, p_tiles * tq, LANES), group_map)
    return pl.pallas_call(
        functools.partial(_attn_kernel, tq=tq, kc=kc, n_ctx=n_ctx, n_own=n_own, n_q=n_q, n_tiles=n_tiles,
                          tiles_per_step=p_tiles, lam_init=lam_init),
        grid=(n_tiles // p_tiles + 1,),
        in_specs=[group_spec, pl.BlockSpec((None, tq, LANES), newest_map),
                  head_rows(t), head_rows(t), ctx_spec, ctx_spec,
                  pl.BlockSpec((8, LANES), lambda i: (0, 0)),
                  pl.BlockSpec((V_DIM_B, 1), lambda i: (0, 0))],
        out_specs=group_spec,
        out_shape=jax.ShapeDtypeStruct((b, t, D_B), BF16),
        scratch_shapes=[pltpu.VMEM((2, n_chunks, kc, LANES), BF16),
                        pltpu.VMEM((2, n_chunks, V_DIM_B, kc), BF16),
                        pltpu.VMEM((n_chunks, kc, 2 * tq), F32),
                        pltpu.VMEM((n_chunks, kc, 2 * tq), F32),
                        pltpu.VMEM((2, 1, 2 * tq), F32)],
        compiler_params=_params(1),
        name="attn",
    )(q, q, k, v, ctx_k, ctx_v, lam, g_sub_col)


def _attn_seq_kernel(q_ref, k_ref, v_ref, lam_ref, gsub_ref, o_ref, *, t, lam_init):
    lam = lam_ref[0:1, 0:1]
    g_col = gsub_ref[...]
    heads = [slice(h * LANES, (h + 1) * LANES) for h in range(N_HEADS_B)]
    s = [_scores_t(k_ref[:, cols], _two_map_queries(q_ref[:, cols])) for cols in heads]
    v_t = [v_ref[:, cols].astype(F32).T.astype(BF16) for cols in heads]
    p = [jnp.exp2(sh - jnp.max(sh, axis=0, keepdims=True)) for sh in s]
    l = [jnp.sum(ph, axis=0, keepdims=True) for ph in p]
    acc = [_dot(vh, ph.astype(BF16)) for vh, ph in zip(v_t, p)]
    for cols, acc_h, l_h in zip(heads, acc, l):
        o_ref[:, cols] = _diff_head_out(acc_h, l_h, lam, g_col, t, lam_init)


def _attn_seq_call(q, k, v, lam, g_sub_col, *, lam_init):
    b, t, _ = q.shape
    seq_block = pl.BlockSpec((None, t, D_B), lambda bi: (bi, 0, 0))
    return pl.pallas_call(
        functools.partial(_attn_seq_kernel, t=t, lam_init=lam_init),
        grid=(b,),
        in_specs=[seq_block, seq_block, seq_block,
                  pl.BlockSpec((8, LANES), lambda bi: (0, 0)),
                  pl.BlockSpec((V_DIM_B, 1), lambda bi: (0, 0))],
        out_specs=seq_block,
        out_shape=jax.ShapeDtypeStruct((b, t, D_B), BF16),
        compiler_params=_params(1),
        name="attn_seq",
    )(q, k, v, lam, g_sub_col)


def _mix_kernel(gta_ref, o_ref, sgb_ref, x_ref, mod_ref, wb_ref, wo_ref, gpost_ref, x1_ref):
    merged = gta_ref[...] + sgb_ref[...] * _dot(o_ref[...], wb_ref[...])
    mix = _dot(merged.astype(BF16), wo_ref[...])
    g1 = mod_ref[2:3, :]
    x1_ref[...] = x_ref[...] + g1 * _rms(mix, gpost_ref[...])


def _mix_call(gta, o, sgb, x, mod, w_b, w_o, g_post, *, seq, tm):
    n = x.shape[0]
    row = lambda w: pl.BlockSpec((tm, w), lambda i: (i, 0))
    return pl.pallas_call(
        _mix_kernel,
        grid=(n // tm,),
        in_specs=[row(D_MODEL), row(D_B), row(D_MODEL), row(D_MODEL), _mod_spec(mod, tm, seq),
                  _const_spec((D_B, D_MODEL)), _const_spec((D_MODEL, D_MODEL)), _const_spec((1, D_MODEL))],
        out_specs=row(D_MODEL),
        out_shape=jax.ShapeDtypeStruct((n, D_MODEL), F32),
        compiler_params=_params(1),
        name="mix",
    )(gta, o, sgb, x, mod, w_b, w_o, g_post)


FFN_HALO = BF16_SUBLANES


def _ffn_kernel(*refs, tm, seq, halo, n_col_chunks):
    refs = list(refs)
    x_ref = refs[0]
    pos = 1
    if halo:
        xp_ref, xn_ref = refs[1:3]
        pos = 3
    mod_ref, gpre_ref, wup_ref, cw_ref, cb_ref, wdn_ref, gpost_ref, out_ref, hbuf = refs[pos:pos + 9]

    mod = mod_ref[...]
    sh2, sc2, g2 = mod[3:4], mod[4:5], mod[5:6]

    def pre(xv):
        return (_rms(xv, gpre_ref[...]) * (1.0 + sc2) + sh2).astype(BF16)

    x = x_ref[...]
    lo = FFN_HALO if halo else 0
    rows_all = tm + 2 * lo
    if halo:
        hbuf[0:lo] = pre(xp_ref[...])
        hbuf[lo + tm:rows_all] = pre(xn_ref[...])
    hbuf[lo:lo + tm] = pre(x)

    t_in_seq = (pl.program_id(0) * tm + lax.broadcasted_iota(jnp.int32, (tm, 1), 0)) & (seq - 1)
    is_first = t_in_seq == 0
    is_last = t_in_seq == seq - 1

    cw_chunk = D_FF // n_col_chunks
    acc = None
    for c in range(n_col_chunks):
        ca = c * cw_chunk
        a = _dot(hbuf[...], wup_ref[:, ca:ca + cw_chunk])
        gate = _dot(hbuf[lo:lo + tm], wup_ref[:, D_FF + ca:D_FF + ca + cw_chunk])
        a_prev = pltpu.roll(a, 1, 0)[lo:lo + tm]
        a_next = pltpu.roll(a, rows_all - 1, 0)[lo:lo + tm]
        a_mid = a[lo:lo + tm]
        a_prev = jnp.where(is_first, 0.0, a_prev)
        a_next = jnp.where(is_last, 0.0, a_next)
        cw = cw_ref[:, ca:ca + cw_chunk]
        conv = cw[0:1] * a_prev + cw[1:2] * a_mid + cw[2:3] * a_next + cb_ref[:, ca:ca + cw_chunk]
        act = (jax.nn.gelu(conv) * gate).astype(BF16)
        y = _dot(act, wdn_ref[ca:ca + cw_chunk, :])
        acc = y if acc is None else acc + y
    out_ref[...] = x + g2 * _rms(acc, gpost_ref[...])


def _ffn_call(x1, mod, g_pre, w_up, conv_w, conv_b, w_down, g_post, *, seq, tm, n_col_chunks):
    n = x1.shape[0]
    halo = tm % seq != 0
    row = pl.BlockSpec((tm, D_MODEL), lambda i: (i, 0))
    in_specs = [row]
    args = [x1]
    if halo:
        hb = tm // FFN_HALO
        last = n // FFN_HALO - 1
        in_specs += [pl.BlockSpec((FFN_HALO, D_MODEL), lambda i: (jnp.maximum(i * hb - 1, 0), 0)),
                     pl.BlockSpec((FFN_HALO, D_MODEL), lambda i: (jnp.minimum((i + 1) * hb, last), 0))]
        args += [x1, x1]
    in_specs += [_mod_spec(mod, tm, seq),
                 _const_spec((1, D_MODEL)), _const_spec((D_MODEL, 2 * D_FF)), _const_spec((3, D_FF)),
                 _const_spec((1, D_FF)), _const_spec((D_FF, D_MODEL)), _const_spec((1, D_MODEL))]
    args += [mod, g_pre, w_up, conv_w, conv_b, w_down, g_post]
    rows_all = tm + (2 * FFN_HALO if halo else 0)
    return pl.pallas_call(
        functools.partial(_ffn_kernel, tm=tm, seq=seq, halo=halo, n_col_chunks=n_col_chunks),
        grid=(n // tm,),
        in_specs=in_specs,
        out_specs=row,
        out_shape=jax.ShapeDtypeStruct((n, D_MODEL), F32),
        scratch_shapes=[pltpu.VMEM((rows_all, D_MODEL), BF16)],
        compiler_params=_params(1),
        name="ffn",
    )(*args)


def _rope_tables(n_tok):
    pos = jnp.arange(n_tok)
    row = (pos // GRID_W).astype(F32)
    col = (pos % GRID_W).astype(F32)
    n_freq = HEAD_DIM_B // 4
    inv = ROPE_THETA ** (-jnp.arange(n_freq, dtype=F32) / n_freq)
    ang = jnp.concatenate([row[:, None] * inv, col[:, None] * inv], axis=-1)
    cos, sin = jnp.cos(ang), jnp.sin(ang)
    cos_t = jnp.tile(cos, (1, LANES // (HEAD_DIM_B // 2)))
    sin_t = jnp.tile(jnp.concatenate([-sin, sin], axis=-1), (1, LANES // HEAD_DIM_B))
    return cos_t, sin_t


def _group(x, mod, p, lam, rope_tabs, ctx_k, ctx_v, lam_init, *, tm_proj, tm_mix, tm_ffn, tq, kc, emit_kv):
    b, t, _ = x.shape
    xf = x.reshape(b * t, D_MODEL)
    outs = _inproj_call(xf, mod, p["g_pre_mix"], p["w_in"], p["g_sgu"], p["w_s"], p["b_s_t"], p["w_a"], rope_tabs,
                        seq=t, tm=tm_proj, emit_kv=emit_kv)
    gta, q, k, v, sgb = outs[:5]
    q, k, v = q.reshape(b, t, D_QK), k.reshape(b, t, D_QK), v.reshape(b, t, D_B)
    if ctx_k is None:
        o = _attn_seq_call(q, k, v, lam, p["g_subln_col"], lam_init=lam_init)
    else:
        o = _attn_call(q, k, v, ctx_k, ctx_v, lam, p["g_subln_col"], tq=tq, kc=kc, tiles_per_step=2,
                       lam_init=lam_init)
    x1 = _mix_call(gta, o.reshape(b * t, D_B), sgb, xf, mod, p["w_b"], p["w_o"], p["g_post_mix"], seq=t, tm=tm_mix)
    y = _ffn_call(x1, mod, p["g_pre_ffn"], p["w_up"], p["conv_w"], p["conv_b"], p["w_down"], p["g_post_ffn"],
                  seq=t, tm=tm_ffn, n_col_chunks=1)
    kv = outs[5:] if emit_kv else None
    return y.reshape(b, t, D_MODEL), kv


def kernel(x_prompt, x_sample, c, cache_k, cache_v, c_ctx, w_ada, b_ada, g_pre_mix, g_post_mix, g_pre_ffn,
           g_post_ffn, w_in, g_sgu, w_s, b_s, lam_q1, lam_k1, lam_q2, lam_k2, g_subln, w_a, w_b, w_o, w_up,
           conv_w, conv_b, w_down):
    depth = w_in.shape[0]
    assert depth == 1, "single-layer configuration"
    l = 0
    lam_init = _lambda_init(l)
    nb, ns = x_prompt.shape[0], x_sample.shape[0]
    t_s = x_sample.shape[1]

    p = {
        "g_pre_mix": g_pre_mix[l][None], "g_post_mix": g_post_mix[l][None],
        "g_pre_ffn": g_pre_ffn[l][None], "g_post_ffn": g_post_ffn[l][None],
        "w_in": w_in[l].astype(BF16), "g_sgu": g_sgu[l][None],
        "w_s": w_s[l].astype(BF16), "b_s_t": b_s[l].T,
        "g_subln_col": g_subln[l][:, None],
        "w_a": w_a[l].astype(BF16), "w_b": w_b[l].astype(BF16), "w_o": w_o[l].astype(BF16),
        "w_up": w_up[l].astype(BF16), "conv_w": conv_w[l], "conv_b": conv_b[l][None],
        "w_down": w_down[l].astype(BF16),
    }

    cond = jnp.concatenate([c_ctx[None], c, jnp.zeros((8 - 1 - ns, D_MODEL), F32)], axis=0)
    mod, lam = _ada_call(cond, w_ada[l], b_ada[l][None], lam_q1[l][None], lam_k1[l][None], lam_q2[l][None],
                         lam_k2[l][None], lam_init)
    mod = mod.reshape(8, 6, D_MODEL)
    mod_ctx, mod_s = mod[0:1], mod[1:1 + ns]

    yp, kv = _group(x_prompt, mod_ctx, p, lam, None, None, None, lam_init,
                    tm_proj=512, tm_mix=512, tm_ffn=512, tq=256, kc=256, emit_kv=True)
    past = cache_k.shape[2]
    ctx_k = cache_k.reshape(ns, past * N_HEADS_B, 2 * HEAD_DIM_B)
    ctx_v = cache_v.reshape(ns, past * N_HEADS_B, V_DIM_B)
    ys, _ = _group(x_sample, mod_s, p, lam, _rope_tables(t_s), ctx_k, ctx_v, lam_init,
                   tm_proj=512, tm_mix=512, tm_ffn=512, tq=256, kc=512, emit_kv=False)

    seq = x_prompt.shape[1]
    new_k = kv[0].reshape(nb, 1, seq, N_HEADS_B, 2 * HEAD_DIM_B)
    new_v = kv[1].reshape(nb, 1, seq, N_HEADS_B, V_DIM_B)
    return (yp, ys, new_k, new_v)
```

```python
import functools
import math

import jax
import jax.numpy as jnp
from jax import lax
from jax.experimental import pallas as pl
from jax.experimental.pallas import tpu as pltpu

D_MODEL = 1024
GRID_W = 64
CHUNK = 128
N_GROUPS_A = 4
D_A = 512
N_HEADS_B = 8
HEAD_DIM_B = 64
V_DIM_B = 2 * HEAD_DIM_B
D_B = N_HEADS_B * V_DIM_B
D_QK = N_HEADS_B * 2 * HEAD_DIM_B
D_FF = 2816
ROPE_THETA = 10000.0
EPS = 1e-6
D_IN = 2 * D_A + 2 * D_QK + D_B + 2 * D_MODEL
OFF_U, OFF_VA, OFF_Q, OFF_K, OFF_V, OFF_GA, OFF_GB = 0, D_A, 2 * D_A, 2 * D_A + D_QK, 2 * D_A + 2 * D_QK, \
    2 * D_A + 2 * D_QK + D_B, 2 * D_A + 2 * D_QK + D_B + D_MODEL

Q_SCALE = HEAD_DIM_B ** -0.5 * math.log2(math.e)

LANES = 128
BF16_SUBLANES = 16
VMEM_LIMIT_BYTES = 56 * 1024 * 1024

BF16 = jnp.bfloat16
F32 = jnp.float32


def _lambda_init(layer_idx):
    return 0.8 - 0.6 * math.exp(-0.3 * layer_idx)


def _rms(x, g):
    r = lax.rsqrt(jnp.mean(x * x, axis=-1, keepdims=True) + EPS)
    return (x * r) * g


def _gelu(x):
    c = 2.0 * math.sqrt(2.0 / math.pi)
    return x * jax.nn.sigmoid(x * (c + (c * 0.044715) * (x * x)))


def _dot(a, b):
    return jnp.dot(a, b, preferred_element_type=F32)


def _const_spec(shape):
    nd = len(shape)
    return pl.BlockSpec(shape, lambda *_: (0,) * nd, pipeline_mode=pl.Buffered(1))


def _mod_spec(mod, tm, seq):
    if mod.shape[0] == 1:
        return pl.BlockSpec((None, 6, D_MODEL), lambda i: (0, 0, 0))
    tiles_per_seq = seq // tm
    return pl.BlockSpec((None, 6, D_MODEL), lambda i: (i // tiles_per_seq, 0, 0))


def _params(n_axes):
    return pltpu.CompilerParams(dimension_semantics=("arbitrary",) * n_axes,
                                vmem_limit_bytes=VMEM_LIMIT_BYTES)


def _ada_kernel(cond_ref, w_ref, b_ref, lq1_ref, lk1_ref, lq2_ref, lk2_ref, mod_ref, lam_ref, *, lam_init):
    c = cond_ref[...]
    s = (c * jax.nn.sigmoid(c)).astype(BF16)
    mod_ref[...] = _dot(s, w_ref[...].astype(BF16)) + b_ref[...]
    d1 = jnp.sum(lq1_ref[...] * lk1_ref[...], axis=-1, keepdims=True)
    d2 = jnp.sum(lq2_ref[...] * lk2_ref[...], axis=-1, keepdims=True)
    lam = jnp.exp(d1) - jnp.exp(d2) + lam_init
    lam_ref[...] = jnp.broadcast_to(lam, lam_ref.shape)


def _ada_call(cond, w_ada, b_ada, lq1, lk1, lq2, lk2, lam_init):
    rows = cond.shape[0]
    tn = D_MODEL
    vec = lambda n: pl.BlockSpec((1, n), lambda j: (0, 0))
    return pl.pallas_call(
        functools.partial(_ada_kernel, lam_init=lam_init),
        grid=(6 * D_MODEL // tn,),
        in_specs=[pl.BlockSpec((rows, D_MODEL), lambda j: (0, 0)),
                  pl.BlockSpec((D_MODEL, tn), lambda j: (0, j)),
                  pl.BlockSpec((1, tn), lambda j: (0, j)),
                  vec(HEAD_DIM_B), vec(HEAD_DIM_B), vec(HEAD_DIM_B), vec(HEAD_DIM_B)],
        out_specs=[pl.BlockSpec((rows, tn), lambda j: (0, j)),
                   pl.BlockSpec((8, LANES), lambda j: (0, 0))],
        out_shape=[jax.ShapeDtypeStruct((rows, 6 * D_MODEL), F32),
                   jax.ShapeDtypeStruct((8, LANES), F32)],
        compiler_params=_params(1),
        name="ada",
    )(cond, w_ada, b_ada, lq1, lk1, lq2, lk2)


def _inproj_kernel(*refs, tm, rope, emit_kv):
    refs = list(refs)
    x_ref, mod_ref, gpre_ref, w_ref, gsgu_ref, ws_ref, bs_ref, wa_ref = refs[:8]
    pos = 8
    if rope:
        cos_ref, sin_ref = refs[pos:pos + 2]
        pos += 2
    gta_ref, q_ref, k_ref, v_ref, sgb_ref = refs[pos:pos + 5]
    pos += 5
    if emit_kv:
        kf_ref, vf_ref = refs[pos:pos + 2]
        pos += 2
    h_scr, a_scr = refs[pos:pos + 2]

    mod = mod_ref[...]
    sh1, sc1 = mod[0:1], mod[1:2]
    h_scr[...] = (_rms(x_ref[...], gpre_ref[...]) * (1.0 + sc1) + sh1).astype(BF16)

    def proj(off, width):
        return _dot(h_scr[...], w_ref[:, off:off + width])

    if rope:
        cos_t, sin_t = cos_ref[...], sin_ref[...]
        lane = lax.broadcasted_iota(jnp.int32, (tm, LANES), 1)
        first_half = (lane & (HEAD_DIM_B // 2)) == 0

    def rotary(xh):
        partner = jnp.where(first_half, pltpu.roll(xh, LANES - HEAD_DIM_B // 2, 1),
                            pltpu.roll(xh, HEAD_DIM_B // 2, 1))
        return xh * cos_t + partner * sin_t

    v = proj(OFF_V, D_B)
    if emit_kv:
        vf_ref[...] = v
    v_ref[...] = v.astype(BF16)

    ug = _gelu(proj(OFF_U, D_A))

    q = proj(OFF_Q, D_QK)
    for hb in range(N_HEADS_B):
        cols = slice(hb * LANES, (hb + 1) * LANES)
        qh = q[:, cols]
        if rope:
            qh = rotary(qh)
        q_ref[:, cols] = (qh * Q_SCALE).astype(BF16)

    vn = _rms(_gelu(proj(OFF_VA, D_A)), gsgu_ref[...]).astype(BF16)

    k = proj(OFF_K, D_QK)
    if emit_kv:
        kf_ref[...] = k
    for hb in range(N_HEADS_B):
        cols = slice(hb * LANES, (hb + 1) * LANES)
        kh = k[:, cols]
        if rope:
            kh = rotary(kh)
        k_ref[:, cols] = kh.astype(BF16)

    for j in range(tm // CHUNK):
        rows = slice(j * CHUNK, (j + 1) * CHUNK)
        for g in range(N_GROUPS_A):
            cols = slice(g * LANES, (g + 1) * LANES)
            mixed = _dot(ws_ref[g], vn[rows, cols]) + bs_ref[:, g:g + 1]
            a_scr[rows, cols] = (ug[rows, cols] * mixed).astype(BF16)

    sgb_ref[...] = jax.nn.sigmoid(proj(OFF_GB, D_MODEL))
    gta_ref[...] = jax.nn.sigmoid(proj(OFF_GA, D_MODEL)) * _dot(a_scr[...], wa_ref[...])


def _inproj_call(x, mod, g_pre, w_in, g_sgu, w_s, b_s_t, w_a, rope_tabs, *, seq, tm, emit_kv):
    n = x.shape[0]
    tiles_per_seq = seq // tm
    rope = rope_tabs is not None
    row = lambda w: pl.BlockSpec((tm, w), lambda i: (i, 0))
    in_specs = [row(D_MODEL), _mod_spec(mod, tm, seq),
                _const_spec((1, D_MODEL)), _const_spec((D_MODEL, D_IN)), _const_spec((1, D_A)),
                _const_spec((N_GROUPS_A, CHUNK, CHUNK)), _const_spec((CHUNK, N_GROUPS_A)),
                _const_spec((D_A, D_MODEL))]
    args = [x, mod, g_pre, w_in, g_sgu, w_s, b_s_t, w_a]
    if rope:
        tab = pl.BlockSpec((tm, LANES), lambda i: (i % tiles_per_seq, 0))
        in_specs += [tab, tab]
        args += list(rope_tabs)
    widths = [(D_MODEL, F32), (D_QK, BF16), (D_QK, BF16), (D_B, BF16), (D_MODEL, F32)]
    if emit_kv:
        widths += [(D_QK, F32), (D_B, F32)]
    return pl.pallas_call(
        functools.partial(_inproj_kernel, tm=tm, rope=rope, emit_kv=emit_kv),
        grid=(n // tm,),
        in_specs=in_specs,
        out_specs=[row(w) for w, _ in widths],
        out_shape=[jax.ShapeDtypeStruct((n, w), dt) for w, dt in widths],
        scratch_shapes=[pltpu.VMEM((tm, D_MODEL), BF16), pltpu.VMEM((tm, D_A), BF16)],
        compiler_params=_params(1),
        name="inproj",
    )(*args)


def _two_map_queries(q):
    lane = lax.broadcasted_iota(jnp.int32, q.shape, 1)
    zero = jnp.zeros_like(q)
    return jnp.concatenate([jnp.where(lane < HEAD_DIM_B, q, zero), jnp.where(lane >= HEAD_DIM_B, q, zero)], axis=0)


def _scores_t(k, qz):
    return lax.dot_general(k, qz, (((1,), (1,)), ((), ())), preferred_element_type=F32)


def _diff_head_out(acc, l, lam, g_col, tq, lam_init):
    rl = 1.0 / l
    o = acc[:, :tq] * rl[:, :tq] - lam * (acc[:, tq:] * rl[:, tq:])
    r = lax.rsqrt(jnp.mean(o * o, axis=0, keepdims=True) + EPS)
    y = (o * r) * g_col * (1.0 - lam_init)
    return y.T.astype(BF16)


def _attn_kernel(q_ref, k_ref, v_ref, ck_ref, cv_ref, lam_ref, gsub_ref, o_ref, kall, vt, s_even, s_odd, mbuf,
                 accbuf, lbuf, *, tq, kc, n_ctx, n_own, n_q, n_tiles, lam_init):
    n_chunks = n_ctx + n_own
    i = pl.program_id(0)
    newest = i

    @pl.when(jnp.logical_and(newest % n_q == 0, newest < n_tiles))
    def _stage_keys():
        slot = (newest // n_q) % 2
        head = (newest // n_q) % N_HEADS_B
        for c in range(n_ctx):
            rows = pl.ds(c * kc * N_HEADS_B + head, kc, stride=N_HEADS_B)
            kall[slot, c] = ck_ref[rows, :].astype(BF16)
            vt[slot, c] = cv_ref[rows, :].T.astype(BF16)
        for c in range(n_own):
            rows = slice(c * kc, (c + 1) * kc)
            kall[slot, n_ctx + c] = k_ref[rows, :]
            vt[slot, n_ctx + c] = v_ref[rows, :].astype(F32).T.astype(BF16)

    @pl.when(i == 0)
    def _define_first_reads():
        s_odd[...] = jnp.zeros(s_odd.shape, F32)
        mbuf[1] = jnp.zeros(mbuf.shape[1:], F32)
        accbuf[0] = jnp.zeros(accbuf.shape[1:], F32)
        lbuf[0] = jnp.ones(lbuf.shape[1:], F32)

    s_bufs = (s_even, s_odd)
    lam = lam_ref[0:1, 0:1]
    g_col = gsub_ref[...]
    k_slot = (jnp.minimum(i, n_tiles - 1) // n_q) % 2
    v_slot = (jnp.clip(i - 1, 0, n_tiles - 1) // n_q) % 2

    def step(par):
        qz = _two_map_queries(q_ref[...])
        s_new, s_old = s_bufs[par], s_bufs[1 - par]
        m_old = mbuf[1 - par]
        m = jnp.full((1, 2 * tq), -jnp.inf, F32)
        l = jnp.zeros((1, 2 * tq), F32)
        acc = jnp.zeros((V_DIM_B, 2 * tq), F32)
        for c in range(n_chunks):
            s = _scores_t(kall[k_slot, c], qz)
            s_new[c] = s
            m = jnp.maximum(m, jnp.max(s, axis=0, keepdims=True))
            p = jnp.exp2(s_old[c] - m_old)
            l = l + jnp.sum(p, axis=0, keepdims=True)
            acc = acc + _dot(vt[v_slot, c], p.astype(BF16))
        o_ref[...] = _diff_head_out(accbuf[par], lbuf[par], lam, g_col, tq, lam_init)
        mbuf[par] = m
        accbuf[1 - par] = acc
        lbuf[1 - par] = l

    pl.when(i % 2 == 0)(lambda: step(0))
    pl.when(i % 2 == 1)(lambda: step(1))


def _attn_call(q, k, v, ctx_k, ctx_v, lam, g_sub_col, *, tq, kc, lam_init):
    b, t, _ = q.shape
    n_q = t // tq
    n_tiles = b * N_HEADS_B * n_q
    n_own = t // kc
    ctx_rows = ctx_k.shape[1]
    n_ctx = ctx_rows // N_HEADS_B // kc
    n_chunks = n_own + n_ctx

    def head_of(tile):
        g = tile // n_q
        return g // N_HEADS_B, g % N_HEADS_B

    def tile_map(tile):
        bi, h = head_of(tile)
        return bi, tile % n_q, h

    def head_map(i):
        bi, h = head_of(jnp.minimum(i, n_tiles - 1))
        return bi, 0, h

    def batch_map(i):
        return head_map(i)[0], 0, 0

    head_rows = lambda rows: pl.BlockSpec((None, rows, LANES), head_map)
    ctx_spec = pl.BlockSpec((None, ctx_rows, LANES), batch_map)
    tile_block = (None, tq, LANES)
    return pl.pallas_call(
        functools.partial(_attn_kernel, tq=tq, kc=kc, n_ctx=n_ctx, n_own=n_own, n_q=n_q, n_tiles=n_tiles,
                          lam_init=lam_init),
        grid=(n_tiles + 2,),
        in_specs=[pl.BlockSpec(tile_block, lambda i: tile_map(jnp.minimum(i, n_tiles - 1))),
                  head_rows(t), head_rows(t), ctx_spec, ctx_spec,
                  pl.BlockSpec((8, LANES), lambda i: (0, 0)),
                  pl.BlockSpec((V_DIM_B, 1), lambda i: (0, 0))],
        out_specs=pl.BlockSpec(tile_block, lambda i: tile_map(jnp.maximum(i - 2, 0))),
        out_shape=jax.ShapeDtypeStruct((b, t, D_B), BF16),
        scratch_shapes=[pltpu.VMEM((2, n_chunks, kc, LANES), BF16),
                        pltpu.VMEM((2, n_chunks, V_DIM_B, kc), BF16),
                        pltpu.VMEM((n_chunks, kc, 2 * tq), F32),
                        pltpu.VMEM((n_chunks, kc, 2 * tq), F32),
                        pltpu.VMEM((2, 1, 2 * tq), F32),
                        pltpu.VMEM((2, V_DIM_B, 2 * tq), F32),
                        pltpu.VMEM((2, 1, 2 * tq), F32)],
        compiler_params=_params(1),
        name="attn",
    )(q, k, v, ctx_k, ctx_v, lam, g_sub_col)


def _attn_seq_kernel(q_ref, k_ref, v_ref, lam_ref, gsub_ref, o_ref, *, t, lam_init):
    lam = lam_ref[0:1, 0:1]
    g_col = gsub_ref[...]
    heads = [slice(h * LANES, (h + 1) * LANES) for h in range(N_HEADS_B)]
    s = [_scores_t(k_ref[:, cols], _two_map_queries(q_ref[:, cols])) for cols in heads]
    v_t = [v_ref[:, cols].astype(F32).T.astype(BF16) for cols in heads]
    p = [jnp.exp2(sh - jnp.max(sh, axis=0, keepdims=True)) for sh in s]
    l = [jnp.sum(ph, axis=0, keepdims=True) for ph in p]
    acc = [_dot(vh, ph.astype(BF16)) for vh, ph in zip(v_t, p)]
    for cols, acc_h, l_h in zip(heads, acc, l):
        o_ref[:, cols] = _diff_head_out(acc_h, l_h, lam, g_col, t, lam_init)


def _attn_seq_call(q, k, v, lam, g_sub_col, *, lam_init):
    b, t, _ = q.shape
    seq_block = pl.BlockSpec((None, t, D_B), lambda bi: (bi, 0, 0))
    return pl.pallas_call(
        functools.partial(_attn_seq_kernel, t=t, lam_init=lam_init),
        grid=(b,),
        in_specs=[seq_block, seq_block, seq_block,
                  pl.BlockSpec((8, LANES), lambda bi: (0, 0)),
                  pl.BlockSpec((V_DIM_B, 1), lambda bi: (0, 0))],
        out_specs=seq_block,
        out_shape=jax.ShapeDtypeStruct((b, t, D_B), BF16),
        compiler_params=_params(1),
        name="attn_seq",
    )(q, k, v, lam, g_sub_col)


def _mix_kernel(gta_ref, o_ref, sgb_ref, x_ref, mod_ref, wb_ref, wo_ref, gpost_ref, x1_ref):
    merged = gta_ref[...] + sgb_ref[...] * _dot(o_ref[...], wb_ref[...])
    mix = _dot(merged.astype(BF16), wo_ref[...])
    g1 = mod_ref[2:3, :]
    x1_ref[...] = x_ref[...] + g1 * _rms(mix, gpost_ref[...])


def _mix_call(gta, o, sgb, x, mod, w_b, w_o, g_post, *, seq, tm):
    n = x.shape[0]
    row = lambda w: pl.BlockSpec((tm, w), lambda i: (i, 0))
    return pl.pallas_call(
        _mix_kernel,
        grid=(n // tm,),
        in_specs=[row(D_MODEL), row(D_B), row(D_MODEL), row(D_MODEL), _mod_spec(mod, tm, seq),
                  _const_spec((D_B, D_MODEL)), _const_spec((D_MODEL, D_MODEL)), _const_spec((1, D_MODEL))],
        out_specs=row(D_MODEL),
        out_shape=jax.ShapeDtypeStruct((n, D_MODEL), F32),
        compiler_params=_params(1),
        name="mix",
    )(gta, o, sgb, x, mod, w_b, w_o, g_post)


FFN_HALO = BF16_SUBLANES


def _ffn_kernel(*refs, tm, seq, halo, n_col_chunks):
    refs = list(refs)
    x_ref = refs[0]
    pos = 1
    if halo:
        xp_ref, xn_ref = refs[1:3]
        pos = 3
    mod_ref, gpre_ref, wup_ref, cw_ref, cb_ref, wdn_ref, gpost_ref, out_ref, hbuf = refs[pos:pos + 9]

    mod = mod_ref[...]
    sh2, sc2, g2 = mod[3:4], mod[4:5], mod[5:6]

    def pre(xv):
        return (_rms(xv, gpre_ref[...]) * (1.0 + sc2) + sh2).astype(BF16)

    x = x_ref[...]
    lo = FFN_HALO if halo else 0
    rows_all = tm + 2 * lo
    if halo:
        hbuf[0:lo] = pre(xp_ref[...])
        hbuf[lo + tm:rows_all] = pre(xn_ref[...])
    hbuf[lo:lo + tm] = pre(x)

    t_in_seq = (pl.program_id(0) * tm + lax.broadcasted_iota(jnp.int32, (tm, 1), 0)) & (seq - 1)
    is_first = t_in_seq == 0
    is_last = t_in_seq == seq - 1

    cw_chunk = D_FF // n_col_chunks
    acc = None
    for c in range(n_col_chunks):
        ca = c * cw_chunk
        a = _dot(hbuf[...], wup_ref[:, ca:ca + cw_chunk])
        gate = _dot(hbuf[lo:lo + tm], wup_ref[:, D_FF + ca:D_FF + ca + cw_chunk])
        a_prev = pltpu.roll(a, 1, 0)[lo:lo + tm]
        a_next = pltpu.roll(a, rows_all - 1, 0)[lo:lo + tm]
        a_mid = a[lo:lo + tm]
        a_prev = jnp.where(is_first, 0.0, a_prev)
        a_next = jnp.where(is_last, 0.0, a_next)
        cw = cw_ref[:, ca:ca + cw_chunk]
        conv = cw[0:1] * a_prev + cw[1:2] * a_mid + cw[2:3] * a_next + cb_ref[:, ca:ca + cw_chunk]
        act = (_gelu(conv) * gate).astype(BF16)
        y = _dot(act, wdn_ref[ca:ca + cw_chunk, :])
        acc = y if acc is None else acc + y
    out_ref[...] = x + g2 * _rms(acc, gpost_ref[...])


def _ffn_call(x1, mod, g_pre, w_up, conv_w, conv_b, w_down, g_post, *, seq, tm, n_col_chunks):
    n = x1.shape[0]
    halo = tm % seq != 0
    row = pl.BlockSpec((tm, D_MODEL), lambda i: (i, 0))
    in_specs = [row]
    args = [x1]
    if halo:
        hb = tm // FFN_HALO
        last = n // FFN_HALO - 1
        in_specs += [pl.BlockSpec((FFN_HALO, D_MODEL), lambda i: (jnp.maximum(i * hb - 1, 0), 0)),
                     pl.BlockSpec((FFN_HALO, D_MODEL), lambda i: (jnp.minimum((i + 1) * hb, last), 0))]
        args += [x1, x1]
    in_specs += [_mod_spec(mod, tm, seq),
                 _const_spec((1, D_MODEL)), _const_spec((D_MODEL, 2 * D_FF)), _const_spec((3, D_FF)),
                 _const_spec((1, D_FF)), _const_spec((D_FF, D_MODEL)), _const_spec((1, D_MODEL))]
    args += [mod, g_pre, w_up, conv_w, conv_b, w_down, g_post]
    rows_all = tm + (2 * FFN_HALO if halo else 0)
    return pl.pallas_call(
        functools.partial(_ffn_kernel, tm=tm, seq=seq, halo=halo, n_col_chunks=n_col_chunks),
        grid=(n // tm,),
        in_specs=in_specs,
        out_specs=row,
        out_shape=jax.ShapeDtypeStruct((n, D_MODEL), F32),
        scratch_shapes=[pltpu.VMEM((rows_all, D_MODEL), BF16)],
        compiler_params=_params(1),
        name="ffn",
    )(*args)


def _rope_tables(n_tok):
    pos = jnp.arange(n_tok)
    row = (pos // GRID_W).astype(F32)
    col = (pos % GRID_W).astype(F32)
    n_freq = HEAD_DIM_B // 4
    inv = ROPE_THETA ** (-jnp.arange(n_freq, dtype=F32) / n_freq)
    ang = jnp.concatenate([row[:, None] * inv, col[:, None] * inv], axis=-1)
    cos, sin = jnp.cos(ang), jnp.sin(ang)
    cos_t = jnp.tile(cos, (1, LANES // (HEAD_DIM_B // 2)))
    sin_t = jnp.tile(jnp.concatenate([-sin, sin], axis=-1), (1, LANES // HEAD_DIM_B))
    return cos_t, sin_t


def _group(x, mod, p, lam, rope_tabs, ctx_k, ctx_v, lam_init, *, tm_proj, tm_mix, tm_ffn, tq, kc, emit_kv):
    b, t, _ = x.shape
    xf = x.reshape(b * t, D_MODEL)
    outs = _inproj_call(xf, mod, p["g_pre_mix"], p["w_in"], p["g_sgu"], p["w_s"], p["b_s_t"], p["w_a"], rope_tabs,
                        seq=t, tm=tm_proj, emit_kv=emit_kv)
    gta, q, k, v, sgb = outs[:5]
    q, k, v = q.reshape(b, t, D_QK), k.reshape(b, t, D_QK), v.reshape(b, t, D_B)
    if ctx_k is None:
        o = _attn_seq_call(q, k, v, lam, p["g_subln_col"], lam_init=lam_init)
    else:
        o = _attn_call(q, k, v, ctx_k, ctx_v, lam, p["g_subln_col"], tq=tq, kc=kc, lam_init=lam_init)
    x1 = _mix_call(gta, o.reshape(b * t, D_B), sgb, xf, mod, p["w_b"], p["w_o"], p["g_post_mix"], seq=t, tm=tm_mix)
    y = _ffn_call(x1, mod, p["g_pre_ffn"], p["w_up"], p["conv_w"], p["conv_b"], p["w_down"], p["g_post_ffn"],
                  seq=t, tm=tm_ffn, n_col_chunks=1)
    kv = outs[5:] if emit_kv else None
    return y.reshape(b, t, D_MODEL), kv


def kernel(x_prompt, x_sample, c, cache_k, cache_v, c_ctx, w_ada, b_ada, g_pre_mix, g_post_mix, g_pre_ffn,
           g_post_ffn, w_in, g_sgu, w_s, b_s, lam_q1, lam_k1, lam_q2, lam_k2, g_subln, w_a, w_b, w_o, w_up,
           conv_w, conv_b, w_down):
    depth = w_in.shape[0]
    assert depth == 1, "single-layer configuration"
    l = 0
    lam_init = _lambda_init(l)
    nb, ns = x_prompt.shape[0], x_sample.shape[0]
    t_s = x_sample.shape[1]

    p = {
        "g_pre_mix": g_pre_mix[l][None], "g_post_mix": g_post_mix[l][None],
        "g_pre_ffn": g_pre_ffn[l][None], "g_post_ffn": g_post_ffn[l][None],
        "w_in": w_in[l].astype(BF16), "g_sgu": g_sgu[l][None],
        "w_s": w_s[l].astype(BF16), "b_s_t": b_s[l].T,
        "g_subln_col": g_subln[l][:, None],
        "w_a": w_a[l].astype(BF16), "w_b": w_b[l].astype(BF16), "w_o": w_o[l].astype(BF16),
        "w_up": w_up[l].astype(BF16), "conv_w": conv_w[l], "conv_b": conv_b[l][None],
        "w_down": w_down[l].astype(BF16),
    }

    cond = jnp.concatenate([c_ctx[None], c, jnp.zeros((8 - 1 - ns, D_MODEL), F32)], axis=0)
    mod, lam = _ada_call(cond, w_ada[l], b_ada[l][None], lam_q1[l][None], lam_k1[l][None], lam_q2[l][None],
                         lam_k2[l][None], lam_init)
    mod = mod.reshape(8, 6, D_MODEL)
    mod_ctx, mod_s = mod[0:1], mod[1:1 + ns]

    yp, kv = _group(x_prompt, mod_ctx, p, lam, None, None, None, lam_init,
                    tm_proj=512, tm_mix=512, tm_ffn=512, tq=256, kc=256, emit_kv=True)
    past = cache_k.shape[2]
    ctx_k = cache_k.reshape(ns, past * N_HEADS_B, 2 * HEAD_DIM_B)
    ctx_v = cache_v.reshape(ns, past * N_HEADS_B, V_DIM_B)
    ys, _ = _group(x_sample, mod_s, p, lam, _rope_tables(t_s), ctx_k, ctx_v, lam_init,
                   tm_proj=512, tm_mix=512, tm_ffn=512, tq=256, kc=512, emit_kv=False)

    seq = x_prompt.shape[1]
    new_k = kv[0].reshape(nb, 1, seq, N_HEADS_B, 2 * HEAD_DIM_B)
    new_v = kv[1].reshape(nb, 1, seq, N_HEADS_B, V_DIM_B)
    return (yp, ys, new_k, new_v)
```

```python
import functools
import math

import jax
import jax.numpy as jnp
from jax import lax
from jax.experimental import pallas as pl
from jax.experimental.pallas import tpu as pltpu

D_MODEL = 1024
GRID_W = 64
CHUNK = 128
N_GROUPS_A = 4
D_A = 512
N_HEADS_B = 8
HEAD_DIM_B = 64
V_DIM_B = 2 * HEAD_DIM_B
D_B = N_HEADS_B * V_DIM_B
D_QK = N_HEADS_B * 2 * HEAD_DIM_B
D_FF = 2816
ROPE_THETA = 10000.0
EPS = 1e-6
D_IN = 2 * D_A + 2 * D_QK + D_B + 2 * D_MODEL
OFF_U, OFF_VA, OFF_Q, OFF_K, OFF_V, OFF_GA, OFF_GB = 0, D_A, 2 * D_A, 2 * D_A + D_QK, 2 * D_A + 2 * D_QK, \
    2 * D_A + 2 * D_QK + D_B, 2 * D_A + 2 * D_QK + D_B + D_MODEL

Q_SCALE = HEAD_DIM_B ** -0.5 * math.log2(math.e)

LANES = 128
BF16_SUBLANES = 16
VMEM_LIMIT_BYTES = 56 * 1024 * 1024

BF16 = jnp.bfloat16
F32 = jnp.float32


def _lambda_init(layer_idx):
    return 0.8 - 0.6 * math.exp(-0.3 * layer_idx)


def _rms(x, g):
    r = lax.rsqrt(jnp.mean(x * x, axis=-1, keepdims=True) + EPS)
    return (x * r) * g


def _gelu(x):
    c = 2.0 * math.sqrt(2.0 / math.pi)
    return x * jax.nn.sigmoid(x * (c + (c * 0.044715) * (x * x)))


def _dot(a, b):
    return jnp.dot(a, b, preferred_element_type=F32)


def _const_spec(shape):
    nd = len(shape)
    return pl.BlockSpec(shape, lambda *_: (0,) * nd, pipeline_mode=pl.Buffered(1))


def _mod_spec(mod, tm, seq):
    if mod.shape[0] == 1:
        return pl.BlockSpec((None, 6, D_MODEL), lambda i: (0, 0, 0))
    tiles_per_seq = seq // tm
    return pl.BlockSpec((None, 6, D_MODEL), lambda i: (i // tiles_per_seq, 0, 0))


def _params(n_axes):
    return pltpu.CompilerParams(dimension_semantics=("arbitrary",) * n_axes,
                                vmem_limit_bytes=VMEM_LIMIT_BYTES)


def _ada_kernel(cond_ref, w_ref, b_ref, lq1_ref, lk1_ref, lq2_ref, lk2_ref, mod_ref, lam_ref, *, lam_init):
    c = cond_ref[...]
    s = (c * jax.nn.sigmoid(c)).astype(BF16)
    mod_ref[...] = _dot(s, w_ref[...].astype(BF16)) + b_ref[...]
    d1 = jnp.sum(lq1_ref[...] * lk1_ref[...], axis=-1, keepdims=True)
    d2 = jnp.sum(lq2_ref[...] * lk2_ref[...], axis=-1, keepdims=True)
    lam = jnp.exp(d1) - jnp.exp(d2) + lam_init
    lam_ref[...] = jnp.broadcast_to(lam, lam_ref.shape)


def _ada_call(cond, w_ada, b_ada, lq1, lk1, lq2, lk2, lam_init):
    rows = cond.shape[0]
    tn = D_MODEL
    vec = lambda n: pl.BlockSpec((1, n), lambda j: (0, 0))
    return pl.pallas_call(
        functools.partial(_ada_kernel, lam_init=lam_init),
        grid=(6 * D_MODEL // tn,),
        in_specs=[pl.BlockSpec((rows, D_MODEL), lambda j: (0, 0)),
                  pl.BlockSpec((D_MODEL, tn), lambda j: (0, j)),
                  pl.BlockSpec((1, tn), lambda j: (0, j)),
                  vec(HEAD_DIM_B), vec(HEAD_DIM_B), vec(HEAD_DIM_B), vec(HEAD_DIM_B)],
        out_specs=[pl.BlockSpec((rows, tn), lambda j: (0, j)),
                   pl.BlockSpec((8, LANES), lambda j: (0, 0))],
        out_shape=[jax.ShapeDtypeStruct((rows, 6 * D_MODEL), F32),
                   jax.ShapeDtypeStruct((8, LANES), F32)],
        compiler_params=_params(1),
        name="ada",
    )(cond, w_ada, b_ada, lq1, lk1, lq2, lk2)


def _inproj_kernel(*refs, tm, rope, emit_kv):
    refs = list(refs)
    x_ref, mod_ref, gpre_ref, w_ref, gsgu_ref, ws_ref, bs_ref, wa_ref = refs[:8]
    pos = 8
    if rope:
        cos_ref, sin_ref = refs[pos:pos + 2]
        pos += 2
    gta_ref, q_ref, k_ref, v_ref, sgb_ref = refs[pos:pos + 5]
    pos += 5
    if emit_kv:
        kf_ref, vf_ref = refs[pos:pos + 2]
        pos += 2
    h_scr, a_scr = refs[pos:pos + 2]

    mod = mod_ref[...]
    sh1, sc1 = mod[0:1], mod[1:2]
    h_scr[...] = (_rms(x_ref[...], gpre_ref[...]) * (1.0 + sc1) + sh1).astype(BF16)

    def proj(off, width):
        return _dot(h_scr[...], w_ref[:, off:off + width])

    if rope:
        cos_t, sin_t = cos_ref[...], sin_ref[...]
        lane = lax.broadcasted_iota(jnp.int32, (tm, LANES), 1)
        first_half = (lane & (HEAD_DIM_B // 2)) == 0

    def rotary(xh):
        partner = jnp.where(first_half, pltpu.roll(xh, LANES - HEAD_DIM_B // 2, 1),
                            pltpu.roll(xh, HEAD_DIM_B // 2, 1))
        return xh * cos_t + partner * sin_t

    v = proj(OFF_V, D_B)
    if emit_kv:
        vf_ref[...] = v
    v_ref[...] = v.astype(BF16)

    ug = _gelu(proj(OFF_U, D_A))

    q = proj(OFF_Q, D_QK)
    for hb in range(N_HEADS_B):
        cols = slice(hb * LANES, (hb + 1) * LANES)
        qh = q[:, cols]
        if rope:
            qh = rotary(qh)
        q_ref[:, cols] = (qh * Q_SCALE).astype(BF16)

    vn = _rms(_gelu(proj(OFF_VA, D_A)), gsgu_ref[...]).astype(BF16)

    k = proj(OFF_K, D_QK)
    if emit_kv:
        kf_ref[...] = k
    for hb in range(N_HEADS_B):
        cols = slice(hb * LANES, (hb + 1) * LANES)
        kh = k[:, cols]
        if rope:
            kh = rotary(kh)
        k_ref[:, cols] = kh.astype(BF16)

    for j in range(tm // CHUNK):
        rows = slice(j * CHUNK, (j + 1) * CHUNK)
        for g in range(N_GROUPS_A):
            cols = slice(g * LANES, (g + 1) * LANES)
            mixed = _dot(ws_ref[g], vn[rows, cols]) + bs_ref[:, g:g + 1]
            a_scr[rows, cols] = (ug[rows, cols] * mixed).astype(BF16)

    sgb_ref[...] = jax.nn.sigmoid(proj(OFF_GB, D_MODEL))
    gta_ref[...] = jax.nn.sigmoid(proj(OFF_GA, D_MODEL)) * _dot(a_scr[...], wa_ref[...])


def _inproj_call(x, mod, g_pre, w_in, g_sgu, w_s, b_s_t, w_a, rope_tabs, *, seq, tm, emit_kv):
    n = x.shape[0]
    tiles_per_seq = seq // tm
    rope = rope_tabs is not None
    row = lambda w: pl.BlockSpec((tm, w), lambda i: (i, 0))
    in_specs = [row(D_MODEL), _mod_spec(mod, tm, seq),
                _const_spec((1, D_MODEL)), _const_spec((D_MODEL, D_IN)), _const_spec((1, D_A)),
                _const_spec((N_GROUPS_A, CHUNK, CHUNK)), _const_spec((CHUNK, N_GROUPS_A)),
                _const_spec((D_A, D_MODEL))]
    args = [x, mod, g_pre, w_in, g_sgu, w_s, b_s_t, w_a]
    if rope:
        tab = pl.BlockSpec((tm, LANES), lambda i: (i % tiles_per_seq, 0))
        in_specs += [tab, tab]
        args += list(rope_tabs)
    widths = [(D_MODEL, F32), (D_QK, BF16), (D_QK, BF16), (D_B, BF16), (D_MODEL, F32)]
    if emit_kv:
        widths += [(D_QK, F32), (D_B, F32)]
    return pl.pallas_call(
        functools.partial(_inproj_kernel, tm=tm, rope=rope, emit_kv=emit_kv),
        grid=(n // tm,),
        in_specs=in_specs,
        out_specs=[row(w) for w, _ in widths],
        out_shape=[jax.ShapeDtypeStruct((n, w), dt) for w, dt in widths],
        scratch_shapes=[pltpu.VMEM((tm, D_MODEL), BF16), pltpu.VMEM((tm, D_A), BF16)],
        compiler_params=_params(1),
        name="inproj",
    )(*args)


def _two_map_queries(q):
    lane = lax.broadcasted_iota(jnp.int32, q.shape, 1)
    zero = jnp.zeros_like(q)
    return jnp.concatenate([jnp.where(lane < HEAD_DIM_B, q, zero), jnp.where(lane >= HEAD_DIM_B, q, zero)], axis=0)


def _scores_t(k, qz):
    return lax.dot_general(k, qz, (((1,), (1,)), ((), ())), preferred_element_type=F32)


def _diff_head_out(acc, l, lam, g_col, tq, lam_init):
    rl = 1.0 / l
    o = acc[:, :tq] * rl[:, :tq] - lam * (acc[:, tq:] * rl[:, tq:])
    r = lax.rsqrt(jnp.mean(o * o, axis=0, keepdims=True) + EPS)
    y = (o * r) * g_col * (1.0 - lam_init)
    return y.T.astype(BF16)


def _attn_kernel(qa_ref, qb_ref, k_ref, v_ref, ck_ref, cv_ref, lam_ref, gsub_ref, o_ref, kall, vt, s_even, s_odd,
                 mbuf, *, tq, kc, n_ctx, n_own, n_q, n_tiles, tiles_per_step, lam_init):
    p_tiles = tiles_per_step
    n_chunks = n_ctx + n_own
    i = pl.program_id(0)
    newest = p_tiles * i

    @pl.when(jnp.logical_and(newest % n_q == 0, newest < n_tiles))
    def _stage_keys():
        slot = (newest // n_q) % 2
        head = (newest // n_q) % N_HEADS_B
        for c in range(n_ctx):
            rows = pl.ds(c * kc * N_HEADS_B + head, kc, stride=N_HEADS_B)
            kall[slot, c] = ck_ref[rows, :].astype(BF16)
            vt[slot, c] = cv_ref[rows, :].T.astype(BF16)
        for c in range(n_own):
            rows = slice(c * kc, (c + 1) * kc)
            kall[slot, n_ctx + c] = k_ref[rows, :]
            vt[slot, n_ctx + c] = v_ref[rows, :].astype(F32).T.astype(BF16)

    @pl.when(i == 0)
    def _define_first_read():
        s_even[...] = jnp.zeros(s_even.shape, F32)
        mbuf[0] = jnp.zeros(mbuf.shape[1:], F32)

    s_bufs = (s_even, s_odd)
    lam = lam_ref[0:1, 0:1]
    g_col = gsub_ref[...]
    first = p_tiles * (i - 1)
    for u in range(p_tiles):
        k_slot = (jnp.clip(first + u + 1, 0, n_tiles - 1) // n_q) % 2
        v_slot = (jnp.clip(first + u, 0, n_tiles - 1) // n_q) % 2
        q = qa_ref[(u + 1) * tq:(u + 2) * tq, :] if u < p_tiles - 1 else qb_ref[...]
        qz = _two_map_queries(q)
        s_cur, s_prev = s_bufs[(u + 1) % 2], s_bufs[u % 2]
        m_prev = mbuf[u % 2]
        m = jnp.full((1, 2 * tq), -jnp.inf, F32)
        l = jnp.zeros((1, 2 * tq), F32)
        acc = jnp.zeros((V_DIM_B, 2 * tq), F32)
        for c in range(n_chunks):
            s = _scores_t(kall[k_slot, c], qz)
            s_cur[c] = s
            m = jnp.maximum(m, jnp.max(s, axis=0, keepdims=True))
            p = jnp.exp2(s_prev[c] - m_prev)
            l = l + jnp.sum(p, axis=0, keepdims=True)
            acc = acc + _dot(vt[v_slot, c], p.astype(BF16))
        mbuf[(u + 1) % 2] = m
        o_ref[u * tq:(u + 1) * tq, :] = _diff_head_out(acc, l, lam, g_col, tq, lam_init)


def _attn_call(q, k, v, ctx_k, ctx_v, lam, g_sub_col, *, tq, kc, tiles_per_step, lam_init):
    b, t, _ = q.shape
    n_q = t // tq
    n_tiles = b * N_HEADS_B * n_q
    p_tiles = tiles_per_step
    assert p_tiles % 2 == 0 and n_q % p_tiles == 0
    n_own = t // kc
    ctx_rows = ctx_k.shape[1]
    n_ctx = ctx_rows // N_HEADS_B // kc
    n_chunks = n_own + n_ctx

    def head_of(tile):
        g = tile // n_q
        return g // N_HEADS_B, g % N_HEADS_B

    def group_map(i):
        tile = p_tiles * jnp.maximum(i - 1, 0)
        bi, h = head_of(tile)
        return bi, (tile % n_q) // p_tiles, h

    def newest_map(i):
        tile = jnp.minimum(p_tiles * i, n_tiles - 1)
        bi, h = head_of(tile)
        return bi, tile % n_q, h

    def head_map(i):
        bi, h = head_of(jnp.minimum(p_tiles * i, n_tiles - 1))
        return bi, 0, h

    def batch_map(i):
        return head_map(i)[0], 0, 0

    head_rows = lambda rows: pl.BlockSpec((None, rows, LANES), head_map)
    ctx_spec = pl.BlockSpec((None, ctx_rows, LANES), batch_map)
    group_spec = pl.BlockSpec((None, p_tiles * tq, LANES), group_map)
    return pl.pallas_call(
        functools.partial(_attn_kernel, tq=tq, kc=kc, n_ctx=n_ctx, n_own=n_own, n_q=n_q, n_tiles=n_tiles,
                          tiles_per_step=p_tiles, lam_init=lam_init),
        grid=(n_tiles // p_tiles + 1,),
        in_specs=[group_spec, pl.BlockSpec((None, tq, LANES), newest_map),
                  head_rows(t), head_rows(t), ctx_spec, ctx_spec,
                  pl.BlockSpec((8, LANES), lambda i: (0, 0)),
                  pl.BlockSpec((V_DIM_B, 1), lambda i: (0, 0))],
        out_specs=group_spec,
        out_shape=jax.ShapeDtypeStruct((b, t, D_B), BF16),
        scratch_shapes=[pltpu.VMEM((2, n_chunks, kc, LANES), BF16),
                        pltpu.VMEM((2, n_chunks, V_DIM_B, kc), BF16),
                        pltpu.VMEM((n_chunks, kc, 2 * tq), F32),
                        pltpu.VMEM((n_chunks, kc, 2 * tq), F32),
                        pltpu.VMEM((2, 1, 2 * tq), F32)],
        compiler_params=_params(1),
        name="attn",
    )(q, q, k, v, ctx_k, ctx_v, lam, g_sub_col)


def _attn_seq_kernel(q_ref, k_ref, v_ref, lam_ref, gsub_ref, o_ref, *, t, lam_init):
    lam = lam_ref[0:1, 0:1]
    g_col = gsub_ref[...]
    heads = [slice(h * LANES, (h + 1) * LANES) for h in range(N_HEADS_B)]
    s = [_scores_t(k_ref[:, cols], _two_map_queries(q_ref[:, cols])) for cols in heads]
    v_t = [v_ref[:, cols].astype(F32).T.astype(BF16) for cols in heads]
    p = [jnp.exp2(sh - jnp.max(sh, axis=0, keepdims=True)) for sh in s]
    l = [jnp.sum(ph, axis=0, keepdims=True) for ph in p]
    acc = [_dot(vh, ph.astype(BF16)) for vh, ph in zip(v_t, p)]
    for cols, acc_h, l_h in zip(heads, acc, l):
        o_ref[:, cols] = _diff_head_out(acc_h, l_h, lam, g_col, t, lam_init)


def _attn_seq_call(q, k, v, lam, g_sub_col, *, lam_init):
    b, t, _ = q.shape
    seq_block = pl.BlockSpec((None, t, D_B), lambda bi: (bi, 0, 0))
    return pl.pallas_call(
        functools.partial(_attn_seq_kernel, t=t, lam_init=lam_init),
        grid=(b,),
        in_specs=[seq_block, seq_block, seq_block,
                  pl.BlockSpec((8, LANES), lambda bi: (0, 0)),
                  pl.BlockSpec((V_DIM_B, 1), lambda bi: (0, 0))],
        out_specs=seq_block,
        out_shape=jax.ShapeDtypeStruct((b, t, D_B), BF16),
        compiler_params=_params(1),
        name="attn_seq",
    )(q, k, v, lam, g_sub_col)


def _mix_kernel(gta_ref, o_ref, sgb_ref, x_ref, mod_ref, wb_ref, wo_ref, gpost_ref, x1_ref):
    merged = gta_ref[...] + sgb_ref[...] * _dot(o_ref[...], wb_ref[...])
    mix = _dot(merged.astype(BF16), wo_ref[...])
    g1 = mod_ref[2:3, :]
    x1_ref[...] = x_ref[...] + g1 * _rms(mix, gpost_ref[...])


def _mix_call(gta, o, sgb, x, mod, w_b, w_o, g_post, *, seq, tm):
    n = x.shape[0]
    row = lambda w: pl.BlockSpec((tm, w), lambda i: (i, 0))
    return pl.pallas_call(
        _mix_kernel,
        grid=(n // tm,),
        in_specs=[row(D_MODEL), row(D_B), row(D_MODEL), row(D_MODEL), _mod_spec(mod, tm, seq),
                  _const_spec((D_B, D_MODEL)), _const_spec((D_MODEL, D_MODEL)), _const_spec((1, D_MODEL))],
        out_specs=row(D_MODEL),
        out_shape=jax.ShapeDtypeStruct((n, D_MODEL), F32),
        compiler_params=_params(1),
        name="mix",
    )(gta, o, sgb, x, mod, w_b, w_o, g_post)


FFN_HALO = BF16_SUBLANES


def _ffn_kernel(*refs, tm, seq, halo, n_col_chunks):
    refs = list(refs)
    x_ref = refs[0]
    pos = 1
    if halo:
        xp_ref, xn_ref = refs[1:3]
        pos = 3
    mod_ref, gpre_ref, wup_ref, cw_ref, cb_ref, wdn_ref, gpost_ref, out_ref, hbuf = refs[pos:pos + 9]

    mod = mod_ref[...]
    sh2, sc2, g2 = mod[3:4], mod[4:5], mod[5:6]

    def pre(xv):
        return (_rms(xv, gpre_ref[...]) * (1.0 + sc2) + sh2).astype(BF16)

    x = x_ref[...]
    lo = FFN_HALO if halo else 0
    rows_all = tm + 2 * lo
    if halo:
        hbuf[0:lo] = pre(xp_ref[...])
        hbuf[lo + tm:rows_all] = pre(xn_ref[...])
    hbuf[lo:lo + tm] = pre(x)

    t_in_seq = (pl.program_id(0) * tm + lax.broadcasted_iota(jnp.int32, (tm, 1), 0)) & (seq - 1)
    is_first = t_in_seq == 0
    is_last = t_in_seq == seq - 1

    cw_chunk = D_FF // n_col_chunks
    acc = None
    for c in range(n_col_chunks):
        ca = c * cw_chunk
        a = _dot(hbuf[...], wup_ref[:, ca:ca + cw_chunk])
        gate = _dot(hbuf[lo:lo + tm], wup_ref[:, D_FF + ca:D_FF + ca + cw_chunk])
        a_prev = pltpu.roll(a, 1, 0)[lo:lo + tm]
        a_next = pltpu.roll(a, rows_all - 1, 0)[lo:lo + tm]
        a_mid = a[lo:lo + tm]
        a_prev = jnp.where(is_first, 0.0, a_prev)
        a_next = jnp.where(is_last, 0.0, a_next)
        cw = cw_ref[:, ca:ca + cw_chunk]
        conv = cw[0:1] * a_prev + cw[1:2] * a_mid + cw[2:3] * a_next + cb_ref[:, ca:ca + cw_chunk]
        act = (_gelu(conv) * gate).astype(BF16)
        y = _dot(act, wdn_ref[ca:ca + cw_chunk, :])
        acc = y if acc is None else acc + y
    out_ref[...] = x + g2 * _rms(acc, gpost_ref[...])


def _ffn_call(x1, mod, g_pre, w_up, conv_w, conv_b, w_down, g_post, *, seq, tm, n_col_chunks):
    n = x1.shape[0]
    halo = tm % seq != 0
    row = pl.BlockSpec((tm, D_MODEL), lambda i: (i, 0))
    in_specs = [row]
    args = [x1]
    if halo:
        hb = tm // FFN_HALO
        last = n // FFN_HALO - 1
        in_specs += [pl.BlockSpec((FFN_HALO, D_MODEL), lambda i: (jnp.maximum(i * hb - 1, 0), 0)),
                     pl.BlockSpec((FFN_HALO, D_MODEL), lambda i: (jnp.minimum((i + 1) * hb, last), 0))]
        args += [x1, x1]
    in_specs += [_mod_spec(mod, tm, seq),
                 _const_spec((1, D_MODEL)), _const_spec((D_MODEL, 2 * D_FF)), _const_spec((3, D_FF)),
                 _const_spec((1, D_FF)), _const_spec((D_FF, D_MODEL)), _const_spec((1, D_MODEL))]
    args += [mod, g_pre, w_up, conv_w, conv_b, w_down, g_post]
    rows_all = tm + (2 * FFN_HALO if halo else 0)
    return pl.pallas_call(
        functools.partial(_ffn_kernel, tm=tm, seq=seq, halo=halo, n_col_chunks=n_col_chunks),
        grid=(n // tm,),
        in_specs=in_specs,
        out_specs=row,
        out_shape=jax.ShapeDtypeStruct((n, D_MODEL), F32),
        scratch_shapes=[pltpu.VMEM((rows_all, D_MODEL), BF16)],
        compiler_params=_params(1),
        name="ffn",
    )(*args)


def _rope_tables(n_tok):
    pos = jnp.arange(n_tok)
    row = (pos // GRID_W).astype(F32)
    col = (pos % GRID_W).astype(F32)
    n_freq = HEAD_DIM_B // 4
    inv = ROPE_THETA ** (-jnp.arange(n_freq, dtype=F32) / n_freq)
    ang = jnp.concatenate([row[:, None] * inv, col[:, None] * inv], axis=-1)
    cos, sin = jnp.cos(ang), jnp.sin(ang)
    cos_t = jnp.tile(cos, (1, LANES // (HEAD_DIM_B // 2)))
    sin_t = jnp.tile(jnp.concatenate([-sin, sin], axis=-1), (1, LANES // HEAD_DIM_B))
    return cos_t, sin_t


def _group(x, mod, p, lam, rope_tabs, ctx_k, ctx_v, lam_init, *, tm_proj, tm_mix, tm_ffn, tq, kc, emit_kv):
    b, t, _ = x.shape
    xf = x.reshape(b * t, D_MODEL)
    outs = _inproj_call(xf, mod, p["g_pre_mix"], p["w_in"], p["g_sgu"], p["w_s"], p["b_s_t"], p["w_a"], rope_tabs,
                        seq=t, tm=tm_proj, emit_kv=emit_kv)
    gta, q, k, v, sgb = outs[:5]
    q, k, v = q.reshape(b, t, D_QK), k.reshape(b, t, D_QK), v.reshape(b, t, D_B)
    if ctx_k is None:
        o = _attn_seq_call(q, k, v, lam, p["g_subln_col"], lam_init=lam_init)
    else:
        o = _attn_call(q, k, v, ctx_k, ctx_v, lam, p["g_subln_col"], tq=tq, kc=kc, tiles_per_step=4,
                       lam_init=lam_init)
    x1 = _mix_call(gta, o.reshape(b * t, D_B), sgb, xf, mod, p["w_b"], p["w_o"], p["g_post_mix"], seq=t, tm=tm_mix)
    y = _ffn_call(x1, mod, p["g_pre_ffn"], p["w_up"], p["conv_w"], p["conv_b"], p["w_down"], p["g_post_ffn"],
                  seq=t, tm=tm_ffn, n_col_chunks=1)
    kv = outs[5:] if emit_kv else None
    return y.reshape(b, t, D_MODEL), kv


def kernel(x_prompt, x_sample, c, cache_k, cache_v, c_ctx, w_ada, b_ada, g_pre_mix, g_post_mix, g_pre_ffn,
           g_post_ffn, w_in, g_sgu, w_s, b_s, lam_q1, lam_k1, lam_q2, lam_k2, g_subln, w_a, w_b, w_o, w_up,
           conv_w, conv_b, w_down):
    depth = w_in.shape[0]
    assert depth == 1, "single-layer configuration"
    l = 0
    lam_init = _lambda_init(l)
    nb, ns = x_prompt.shape[0], x_sample.shape[0]
    t_s = x_sample.shape[1]

    p = {
        "g_pre_mix": g_pre_mix[l][None], "g_post_mix": g_post_mix[l][None],
        "g_pre_ffn": g_pre_ffn[l][None], "g_post_ffn": g_post_ffn[l][None],
        "w_in": w_in[l].astype(BF16), "g_sgu": g_sgu[l][None],
        "w_s": w_s[l].astype(BF16), "b_s_t": b_s[l].T,
        "g_subln_col": g_subln[l][:, None],
        "w_a": w_a[l].astype(BF16), "w_b": w_b[l].astype(BF16), "w_o": w_o[l].astype(BF16),
        "w_up": w_up[l].astype(BF16), "conv_w": conv_w[l], "conv_b": conv_b[l][None],
        "w_down": w_down[l].astype(BF16),
    }

    cond = jnp.concatenate([c_ctx[None], c, jnp.zeros((8 - 1 - ns, D_MODEL), F32)], axis=0)
    mod, lam = _ada_call(cond, w_ada[l], b_ada[l][None], lam_q1[l][None], lam_k1[l][None], lam_q2[l][None],
                         lam_k2[l][None], lam_init)
    mod = mod.reshape(8, 6, D_MODEL)
    mod_ctx, mod_s = mod[0:1], mod[1:1 + ns]

    yp, kv = _group(x_prompt, mod_ctx, p, lam, None, None, None, lam_init,
                    tm_proj=512, tm_mix=512, tm_ffn=512, tq=256, kc=256, emit_kv=True)
    past = cache_k.shape[2]
    ctx_k = cache_k.reshape(ns, past * N_HEADS_B, 2 * HEAD_DIM_B)
    ctx_v = cache_v.reshape(ns, past * N_HEADS_B, V_DIM_B)
    ys, _ = _group(x_sample, mod_s, p, lam, _rope_tables(t_s), ctx_k, ctx_v, lam_init,
                   tm_proj=512, tm_mix=512, tm_ffn=512, tq=256, kc=512, emit_kv=False)

    seq = x_prompt.shape[1]
    new_k = kv[0].reshape(nb, 1, seq, N_HEADS_B, 2 * HEAD_DIM_B)
    new_v = kv[1].reshape(nb, 1, seq, N_HEADS_B, V_DIM_B)
    return (yp, ys, new_k, new_v)
```

```python
import functools
import math

import jax
import jax.numpy as jnp
from jax import lax
from jax.experimental import pallas as pl
from jax.experimental.pallas import tpu as pltpu

D_MODEL = 1024
GRID_W = 64
CHUNK = 128
N_GROUPS_A = 4
D_A = 512
N_HEADS_B = 8
HEAD_DIM_B = 64
V_DIM_B = 2 * HEAD_DIM_B
D_B = N_HEADS_B * V_DIM_B
D_QK = N_HEADS_B * 2 * HEAD_DIM_B
D_FF = 2816
ROPE_THETA = 10000.0
EPS = 1e-6
D_IN = 2 * D_A + 2 * D_QK + D_B + 2 * D_MODEL
OFF_U, OFF_VA, OFF_Q, OFF_K, OFF_V, OFF_GA, OFF_GB = 0, D_A, 2 * D_A, 2 * D_A + D_QK, 2 * D_A + 2 * D_QK, \
    2 * D_A + 2 * D_QK + D_B, 2 * D_A + 2 * D_QK + D_B + D_MODEL

Q_SCALE = HEAD_DIM_B ** -0.5 * math.log2(math.e)

LANES = 128
BF16_SUBLANES = 16
VMEM_LIMIT_BYTES = 56 * 1024 * 1024

BF16 = jnp.bfloat16
F32 = jnp.float32


def _lambda_init(layer_idx):
    return 0.8 - 0.6 * math.exp(-0.3 * layer_idx)


def _rms(x, g):
    r = lax.rsqrt(jnp.mean(x * x, axis=-1, keepdims=True) + EPS)
    return (x * r) * g


def _gelu(x):
    c = 2.0 * math.sqrt(2.0 / math.pi)
    return x * jax.nn.sigmoid(x * (c + (c * 0.044715) * (x * x)))


def _dot(a, b):
    return jnp.dot(a, b, preferred_element_type=F32)


def _const_spec(shape):
    nd = len(shape)
    return pl.BlockSpec(shape, lambda *_: (0,) * nd, pipeline_mode=pl.Buffered(1))


def _mod_spec(mod, tm, seq):
    if mod.shape[0] == 1:
        return pl.BlockSpec((None, 6, D_MODEL), lambda i: (0, 0, 0))
    tiles_per_seq = seq // tm
    return pl.BlockSpec((None, 6, D_MODEL), lambda i: (i // tiles_per_seq, 0, 0))


def _params(n_axes):
    return pltpu.CompilerParams(dimension_semantics=("arbitrary",) * n_axes,
                                vmem_limit_bytes=VMEM_LIMIT_BYTES)


def _ada_kernel(cond_ref, w_ref, b_ref, lq1_ref, lk1_ref, lq2_ref, lk2_ref, mod_ref, lam_ref, *, lam_init):
    c = cond_ref[...]
    s = (c * jax.nn.sigmoid(c)).astype(BF16)
    mod_ref[...] = _dot(s, w_ref[...].astype(BF16)) + b_ref[...]
    d1 = jnp.sum(lq1_ref[...] * lk1_ref[...], axis=-1, keepdims=True)
    d2 = jnp.sum(lq2_ref[...] * lk2_ref[...], axis=-1, keepdims=True)
    lam = jnp.exp(d1) - jnp.exp(d2) + lam_init
    lam_ref[...] = jnp.broadcast_to(lam, lam_ref.shape)


def _ada_call(cond, w_ada, b_ada, lq1, lk1, lq2, lk2, lam_init):
    rows = cond.shape[0]
    tn = D_MODEL
    vec = lambda n: pl.BlockSpec((1, n), lambda j: (0, 0))
    return pl.pallas_call(
        functools.partial(_ada_kernel, lam_init=lam_init),
        grid=(6 * D_MODEL // tn,),
        in_specs=[pl.BlockSpec((rows, D_MODEL), lambda j: (0, 0)),
                  pl.BlockSpec((D_MODEL, tn), lambda j: (0, j)),
                  pl.BlockSpec((1, tn), lambda j: (0, j)),
                  vec(HEAD_DIM_B), vec(HEAD_DIM_B), vec(HEAD_DIM_B), vec(HEAD_DIM_B)],
        out_specs=[pl.BlockSpec((rows, tn), lambda j: (0, j)),
                   pl.BlockSpec((8, LANES), lambda j: (0, 0))],
        out_shape=[jax.ShapeDtypeStruct((rows, 6 * D_MODEL), F32),
                   jax.ShapeDtypeStruct((8, LANES), F32)],
        compiler_params=_params(1),
        name="ada",
    )(cond, w_ada, b_ada, lq1, lk1, lq2, lk2)


def _inproj_kernel(*refs, tm, rope, emit_kv):
    refs = list(refs)
    x_ref, mod_ref, gpre_ref, w_ref, gsgu_ref, ws_ref, bs_ref, wa_ref = refs[:8]
    pos = 8
    if rope:
        cos_ref, sin_ref = refs[pos:pos + 2]
        pos += 2
    gta_ref, q_ref, k_ref, v_ref, sgb_ref = refs[pos:pos + 5]
    pos += 5
    if emit_kv:
        kf_ref, vf_ref = refs[pos:pos + 2]
        pos += 2
    h_scr, a_scr = refs[pos:pos + 2]

    mod = mod_ref[...]
    sh1, sc1 = mod[0:1], mod[1:2]
    h_scr[...] = (_rms(x_ref[...], gpre_ref[...]) * (1.0 + sc1) + sh1).astype(BF16)

    def proj(off, width):
        return _dot(h_scr[...], w_ref[:, off:off + width])

    if rope:
        cos_t, sin_t = cos_ref[...], sin_ref[...]
        lane = lax.broadcasted_iota(jnp.int32, (tm, LANES), 1)
        first_half = (lane & (HEAD_DIM_B // 2)) == 0

    def rotary(xh):
        partner = jnp.where(first_half, pltpu.roll(xh, LANES - HEAD_DIM_B // 2, 1),
                            pltpu.roll(xh, HEAD_DIM_B // 2, 1))
        return xh * cos_t + partner * sin_t

    v = proj(OFF_V, D_B)
    if emit_kv:
        vf_ref[...] = v
    v_ref[...] = v.astype(BF16)

    ug = _gelu(proj(OFF_U, D_A))

    q = proj(OFF_Q, D_QK)
    for hb in range(N_HEADS_B):
        cols = slice(hb * LANES, (hb + 1) * LANES)
        qh = q[:, cols]
        if rope:
            qh = rotary(qh)
        q_ref[:, cols] = (qh * Q_SCALE).astype(BF16)

    vn = _rms(_gelu(proj(OFF_VA, D_A)), gsgu_ref[...]).astype(BF16)

    k = proj(OFF_K, D_QK)
    if emit_kv:
        kf_ref[...] = k
    for hb in range(N_HEADS_B):
        cols = slice(hb * LANES, (hb + 1) * LANES)
        kh = k[:, cols]
        if rope:
            kh = rotary(kh)
        k_ref[:, cols] = kh.astype(BF16)

    for j in range(tm // CHUNK):
        rows = slice(j * CHUNK, (j + 1) * CHUNK)
        for g in range(N_GROUPS_A):
            cols = slice(g * LANES, (g + 1) * LANES)
            mixed = _dot(ws_ref[g], vn[rows, cols]) + bs_ref[:, g:g + 1]
            a_scr[rows, cols] = (ug[rows, cols] * mixed).astype(BF16)

    sgb_ref[...] = jax.nn.sigmoid(proj(OFF_GB, D_MODEL))
    gta_ref[...] = jax.nn.sigmoid(proj(OFF_GA, D_MODEL)) * _dot(a_scr[...], wa_ref[...])


def _inproj_call(x, mod, g_pre, w_in, g_sgu, w_s, b_s_t, w_a, rope_tabs, *, seq, tm, emit_kv):
    n = x.shape[0]
    tiles_per_seq = seq // tm
    rope = rope_tabs is not None
    row = lambda w: pl.BlockSpec((tm, w), lambda i: (i, 0))
    in_specs = [row(D_MODEL), _mod_spec(mod, tm, seq),
                _const_spec((1, D_MODEL)), _const_spec((D_MODEL, D_IN)), _const_spec((1, D_A)),
                _const_spec((N_GROUPS_A, CHUNK, CHUNK)), _const_spec((CHUNK, N_GROUPS_A)),
                _const_spec((D_A, D_MODEL))]
    args = [x, mod, g_pre, w_in, g_sgu, w_s, b_s_t, w_a]
    if rope:
        tab = pl.BlockSpec((tm, LANES), lambda i: (i % tiles_per_seq, 0))
        in_specs += [tab, tab]
        args += list(rope_tabs)
    widths = [(D_MODEL, F32), (D_QK, BF16), (D_QK, BF16), (D_B, BF16), (D_MODEL, F32)]
    if emit_kv:
        widths += [(D_QK, F32), (D_B, F32)]
    return pl.pallas_call(
        functools.partial(_inproj_kernel, tm=tm, rope=rope, emit_kv=emit_kv),
        grid=(n // tm,),
        in_specs=in_specs,
        out_specs=[row(w) for w, _ in widths],
        out_shape=[jax.ShapeDtypeStruct((n, w), dt) for w, dt in widths],
        scratch_shapes=[pltpu.VMEM((tm, D_MODEL), BF16), pltpu.VMEM((tm, D_A), BF16)],
        compiler_params=_params(1),
        name="inproj",
    )(*args)


def _two_map_queries(q):
    lane = lax.broadcasted_iota(jnp.int32, q.shape, 1)
    zero = jnp.zeros_like(q)
    return jnp.concatenate([jnp.where(lane < HEAD_DIM_B, q, zero), jnp.where(lane >= HEAD_DIM_B, q, zero)], axis=0)


def _scores_t(k, qz):
    return lax.dot_general(k, qz, (((1,), (1,)), ((), ())), preferred_element_type=F32)


def _diff_head_out(acc, l, lam, g_col, tq, lam_init):
    rl = 1.0 / l
    o = acc[:, :tq] * rl[:, :tq] - lam * (acc[:, tq:] * rl[:, tq:])
    r = lax.rsqrt(jnp.mean(o * o, axis=0, keepdims=True) + EPS)
    y = (o * r) * g_col * (1.0 - lam_init)
    return y.T.astype(BF16)


def _attn_kernel(qa_ref, qb_ref, k_ref, v_ref, ck_ref, cv_ref, lam_ref, gsub_ref, o_ref, kall, vt, s_even, s_odd,
                 mbuf, *, tq, kc, n_ctx, n_own, n_q, n_tiles, tiles_per_step, lam_init):
    p_tiles = tiles_per_step
    n_chunks = n_ctx + n_own
    i = pl.program_id(0)
    newest = p_tiles * i

    @pl.when(jnp.logical_and(newest % n_q == 0, newest < n_tiles))
    def _stage_keys():
        slot = (newest // n_q) % 2
        head = (newest // n_q) % N_HEADS_B
        for c in range(n_ctx):
            rows = pl.ds(c * kc * N_HEADS_B + head, kc, stride=N_HEADS_B)
            kall[slot, c] = ck_ref[rows, :].astype(BF16)
            vt[slot, c] = cv_ref[rows, :].T.astype(BF16)
        for c in range(n_own):
            rows = slice(c * kc, (c + 1) * kc)
            kall[slot, n_ctx + c] = k_ref[rows, :]
            vt[slot, n_ctx + c] = v_ref[rows, :].astype(F32).T.astype(BF16)

    @pl.when(i == 0)
    def _define_first_read():
        s_even[...] = jnp.zeros(s_even.shape, F32)
        mbuf[0] = jnp.zeros(mbuf.shape[1:], F32)

    s_bufs = (s_even, s_odd)
    lam = lam_ref[0:1, 0:1]
    g_col = gsub_ref[...]
    first = p_tiles * (i - 1)
    for u in range(p_tiles):
        k_slot = (jnp.clip(first + u + 1, 0, n_tiles - 1) // n_q) % 2
        v_slot = (jnp.clip(first + u, 0, n_tiles - 1) // n_q) % 2
        q = qa_ref[(u + 1) * tq:(u + 2) * tq, :] if u < p_tiles - 1 else qb_ref[...]
        qz = _two_map_queries(q)
        s_cur, s_prev = s_bufs[(u + 1) % 2], s_bufs[u % 2]
        m_prev = mbuf[u % 2]
        m = jnp.full((1, 2 * tq), -jnp.inf, F32)
        l = jnp.zeros((1, 2 * tq), F32)
        acc = jnp.zeros((V_DIM_B, 2 * tq), F32)
        for c in range(n_chunks):
            s = _scores_t(kall[k_slot, c], qz)
            s_cur[c] = s
            m = jnp.maximum(m, jnp.max(s, axis=0, keepdims=True))
            p = jnp.exp2(s_prev[c] - m_prev)
            l = l + jnp.sum(p, axis=0, keepdims=True)
            acc = acc + _dot(vt[v_slot, c], p.astype(BF16))
        mbuf[(u + 1) % 2] = m
        o_ref[u * tq:(u + 1) * tq, :] = _diff_head_out(acc, l, lam, g_col, tq, lam_init)


def _attn_call(q, k, v, ctx_k, ctx_v, lam, g_sub_col, *, tq, kc, tiles_per_step, lam_init):
    b, t, _ = q.shape
    n_q = t // tq
    n_tiles = b * N_HEADS_B * n_q
    p_tiles = tiles_per_step
    assert p_tiles % 2 == 0 and n_q % p_tiles == 0
    n_own = t // kc
    ctx_rows = ctx_k.shape[1]
    n_ctx = ctx_rows // N_HEADS_B // kc
    n_chunks = n_own + n_ctx

    def head_of(tile):
        g = tile // n_q
        return g // N_HEADS_B, g % N_HEADS_B

    def group_map(i):
        tile = p_tiles * jnp.maximum(i - 1, 0)
        bi, h = head_of(tile)
        return bi, (tile % n_q) // p_tiles, h

    def newest_map(i):
        tile = jnp.minimum(p_tiles * i, n_tiles - 1)
        bi, h = head_of(tile)
        return bi, tile % n_q, h

    def head_map(i):
        bi, h = head_of(jnp.minimum(p_tiles * i, n_tiles - 1))
        return bi, 0, h

    def batch_map(i):
        return head_map(i)[0], 0, 0

    head_rows = lambda rows: pl.BlockSpec((None, rows, LANES), head_map)
    ctx_spec = pl.BlockSpec((None, ctx_rows, LANES), batch_map)
    group_spec = pl.BlockSpec((None, p_tiles * tq, LANES), group_map)
    return pl.pallas_call(
        functools.partial(_attn_kernel, tq=tq, kc=kc, n_ctx=n_ctx, n_own=n_own, n_q=n_q, n_tiles=n_tiles,
                          tiles_per_step=p_tiles, lam_init=lam_init),
        grid=(n_tiles // p_tiles + 1,),
        in_specs=[group_spec, pl.BlockSpec((None, tq, LANES), newest_map),
                  head_rows(t), head_rows(t), ctx_spec, ctx_spec,
                  pl.BlockSpec((8, LANES), lambda i: (0, 0)),
                  pl.BlockSpec((V_DIM_B, 1), lambda i: (0, 0))],
        out_specs=group_spec,
        out_shape=jax.ShapeDtypeStruct((b, t, D_B), BF16),
        scratch_shapes=[pltpu.VMEM((2, n_chunks, kc, LANES), BF16),
                        pltpu.VMEM((2, n_chunks, V_DIM_B, kc), BF16),
                        pltpu.VMEM((n_chunks, kc, 2 * tq), F32),
                        pltpu.VMEM((n_chunks, kc, 2 * tq), F32),
                        pltpu.VMEM((2, 1, 2 * tq), F32)],
        compiler_params=_params(1),
        name="attn",
    )(q, q, k, v, ctx_k, ctx_v, lam, g_sub_col)


def _attn_seq_kernel(q_ref, k_ref, v_ref, lam_ref, gsub_ref, o_ref, *, t, n_seq, lam_init):
    lam = lam_ref[0:1, 0:1]
    g_col = gsub_ref[...]
    units = [(b, slice(h * LANES, (h + 1) * LANES)) for b in range(n_seq) for h in range(N_HEADS_B)]
    s = [_scores_t(k_ref[b, :, cols], _two_map_queries(q_ref[b, :, cols])) for b, cols in units]
    v_t = [v_ref[b, :, cols].astype(F32).T.astype(BF16) for b, cols in units]
    p = [jnp.exp2(sh - jnp.max(sh, axis=0, keepdims=True)) for sh in s]
    l = [jnp.sum(ph, axis=0, keepdims=True) for ph in p]
    acc = [_dot(vh, ph.astype(BF16)) for vh, ph in zip(v_t, p)]
    for (b, cols), acc_h, l_h in zip(units, acc, l):
        o_ref[b, :, cols] = _diff_head_out(acc_h, l_h, lam, g_col, t, lam_init)


def _attn_seq_call(q, k, v, lam, g_sub_col, *, n_seq, lam_init):
    b, t, _ = q.shape
    seq_block = pl.BlockSpec((n_seq, t, D_B), lambda bi: (bi, 0, 0))
    return pl.pallas_call(
        functools.partial(_attn_seq_kernel, t=t, n_seq=n_seq, lam_init=lam_init),
        grid=(b // n_seq,),
        in_specs=[seq_block, seq_block, seq_block,
                  pl.BlockSpec((8, LANES), lambda bi: (0, 0)),
                  pl.BlockSpec((V_DIM_B, 1), lambda bi: (0, 0))],
        out_specs=seq_block,
        out_shape=jax.ShapeDtypeStruct((b, t, D_B), BF16),
        compiler_params=_params(1),
        name="attn_seq",
    )(q, k, v, lam, g_sub_col)


def _mix_kernel(gta_ref, o_ref, sgb_ref, x_ref, mod_ref, wb_ref, wo_ref, gpost_ref, x1_ref):
    merged = gta_ref[...] + sgb_ref[...] * _dot(o_ref[...], wb_ref[...])
    mix = _dot(merged.astype(BF16), wo_ref[...])
    g1 = mod_ref[2:3, :]
    x1_ref[...] = x_ref[...] + g1 * _rms(mix, gpost_ref[...])


def _mix_call(gta, o, sgb, x, mod, w_b, w_o, g_post, *, seq, tm):
    n = x.shape[0]
    row = lambda w: pl.BlockSpec((tm, w), lambda i: (i, 0))
    return pl.pallas_call(
        _mix_kernel,
        grid=(n // tm,),
        in_specs=[row(D_MODEL), row(D_B), row(D_MODEL), row(D_MODEL), _mod_spec(mod, tm, seq),
                  _const_spec((D_B, D_MODEL)), _const_spec((D_MODEL, D_MODEL)), _const_spec((1, D_MODEL))],
        out_specs=row(D_MODEL),
        out_shape=jax.ShapeDtypeStruct((n, D_MODEL), F32),
        compiler_params=_params(1),
        name="mix",
    )(gta, o, sgb, x, mod, w_b, w_o, g_post)


FFN_HALO = BF16_SUBLANES


def _ffn_kernel(*refs, tm, seq, halo, n_col_chunks):
    refs = list(refs)
    x_ref = refs[0]
    pos = 1
    if halo:
        xp_ref, xn_ref = refs[1:3]
        pos = 3
    mod_ref, gpre_ref, wup_ref, cw_ref, cb_ref, wdn_ref, gpost_ref, out_ref, hbuf = refs[pos:pos + 9]

    mod = mod_ref[...]
    sh2, sc2, g2 = mod[3:4], mod[4:5], mod[5:6]

    def pre(xv):
        return (_rms(xv, gpre_ref[...]) * (1.0 + sc2) + sh2).astype(BF16)

    x = x_ref[...]
    lo = FFN_HALO if halo else 0
    rows_all = tm + 2 * lo
    row0 = pl.program_id(0) * tm
    if halo:
        zero_h = jnp.zeros((lo, D_MODEL), BF16)
        hbuf[0:lo] = jnp.where(row0 % seq == 0, zero_h, pre(xp_ref[...]))
        hbuf[lo + tm:rows_all] = jnp.where((row0 + tm) % seq == 0, zero_h, pre(xn_ref[...]))
    else:
        t_in_seq = (row0 + lax.broadcasted_iota(jnp.int32, (tm, 1), 0)) & (seq - 1)
        is_first = t_in_seq == 0
        is_last = t_in_seq == seq - 1
    hbuf[lo:lo + tm] = pre(x)

    cw_chunk = D_FF // n_col_chunks
    acc = None
    for c in range(n_col_chunks):
        ca = c * cw_chunk
        a = _dot(hbuf[...], wup_ref[:, ca:ca + cw_chunk])
        gate = _dot(hbuf[lo:lo + tm], wup_ref[:, D_FF + ca:D_FF + ca + cw_chunk])
        a_prev = pltpu.roll(a, 1, 0)[lo:lo + tm]
        a_next = pltpu.roll(a, rows_all - 1, 0)[lo:lo + tm]
        a_mid = a[lo:lo + tm]
        if not halo:
            a_prev = jnp.where(is_first, 0.0, a_prev)
            a_next = jnp.where(is_last, 0.0, a_next)
        cw = cw_ref[:, ca:ca + cw_chunk]
        conv = cw[0:1] * a_prev + cw[1:2] * a_mid + cw[2:3] * a_next + cb_ref[:, ca:ca + cw_chunk]
        act = (_gelu(conv) * gate).astype(BF16)
        y = _dot(act, wdn_ref[ca:ca + cw_chunk, :])
        acc = y if acc is None else acc + y
    out_ref[...] = x + g2 * _rms(acc, gpost_ref[...])


def _ffn_call(x1, mod, g_pre, w_up, conv_w, conv_b, w_down, g_post, *, seq, tm, n_col_chunks):
    n = x1.shape[0]
    halo = tm % seq != 0
    row = pl.BlockSpec((tm, D_MODEL), lambda i: (i, 0))
    in_specs = [row]
    args = [x1]
    if halo:
        hb = tm // FFN_HALO
        last = n // FFN_HALO - 1
        in_specs += [pl.BlockSpec((FFN_HALO, D_MODEL), lambda i: (jnp.maximum(i * hb - 1, 0), 0)),
                     pl.BlockSpec((FFN_HALO, D_MODEL), lambda i: (jnp.minimum((i + 1) * hb, last), 0))]
        args += [x1, x1]
    in_specs += [_mod_spec(mod, tm, seq),
                 _const_spec((1, D_MODEL)), _const_spec((D_MODEL, 2 * D_FF)), _const_spec((3, D_FF)),
                 _const_spec((1, D_FF)), _const_spec((D_FF, D_MODEL)), _const_spec((1, D_MODEL))]
    args += [mod, g_pre, w_up, conv_w, conv_b, w_down, g_post]
    rows_all = tm + (2 * FFN_HALO if halo else 0)
    return pl.pallas_call(
        functools.partial(_ffn_kernel, tm=tm, seq=seq, halo=halo, n_col_chunks=n_col_chunks),
        grid=(n // tm,),
        in_specs=in_specs,
        out_specs=row,
        out_shape=jax.ShapeDtypeStruct((n, D_MODEL), F32),
        scratch_shapes=[pltpu.VMEM((rows_all, D_MODEL), BF16)],
        compiler_params=_params(1),
        name="ffn",
    )(*args)


def _rope_tables(n_tok):
    pos = jnp.arange(n_tok)
    row = (pos // GRID_W).astype(F32)
    col = (pos % GRID_W).astype(F32)
    n_freq = HEAD_DIM_B // 4
    inv = ROPE_THETA ** (-jnp.arange(n_freq, dtype=F32) / n_freq)
    ang = jnp.concatenate([row[:, None] * inv, col[:, None] * inv], axis=-1)
    cos, sin = jnp.cos(ang), jnp.sin(ang)
    cos_t = jnp.tile(cos, (1, LANES // (HEAD_DIM_B // 2)))
    sin_t = jnp.tile(jnp.concatenate([-sin, sin], axis=-1), (1, LANES // HEAD_DIM_B))
    return cos_t, sin_t


def _group(x, mod, p, lam, rope_tabs, ctx_k, ctx_v, lam_init, *, tm_proj, tm_mix, tm_ffn, tq, kc, emit_kv):
    b, t, _ = x.shape
    xf = x.reshape(b * t, D_MODEL)
    outs = _inproj_call(xf, mod, p["g_pre_mix"], p["w_in"], p["g_sgu"], p["w_s"], p["b_s_t"], p["w_a"], rope_tabs,
                        seq=t, tm=tm_proj, emit_kv=emit_kv)
    gta, q, k, v, sgb = outs[:5]
    q, k, v = q.reshape(b, t, D_QK), k.reshape(b, t, D_QK), v.reshape(b, t, D_B)
    if ctx_k is None:
        o = _attn_seq_call(q, k, v, lam, p["g_subln_col"], n_seq=2, lam_init=lam_init)
    else:
        o = _attn_call(q, k, v, ctx_k, ctx_v, lam, p["g_subln_col"], tq=tq, kc=kc, tiles_per_step=4,
                       lam_init=lam_init)
    x1 = _mix_call(gta, o.reshape(b * t, D_B), sgb, xf, mod, p["w_b"], p["w_o"], p["g_post_mix"], seq=t, tm=tm_mix)
    y = _ffn_call(x1, mod, p["g_pre_ffn"], p["w_up"], p["conv_w"], p["conv_b"], p["w_down"], p["g_post_ffn"],
                  seq=t, tm=tm_ffn, n_col_chunks=1)
    kv = outs[5:] if emit_kv else None
    return y.reshape(b, t, D_MODEL), kv


def kernel(x_prompt, x_sample, c, cache_k, cache_v, c_ctx, w_ada, b_ada, g_pre_mix, g_post_mix, g_pre_ffn,
           g_post_ffn, w_in, g_sgu, w_s, b_s, lam_q1, lam_k1, lam_q2, lam_k2, g_subln, w_a, w_b, w_o, w_up,
           conv_w, conv_b, w_down):
    depth = w_in.shape[0]
    assert depth == 1, "single-layer configuration"
    l = 0
    lam_init = _lambda_init(l)
    nb, ns = x_prompt.shape[0], x_sample.shape[0]
    t_s = x_sample.shape[1]

    p = {
        "g_pre_mix": g_pre_mix[l][None], "g_post_mix": g_post_mix[l][None],
        "g_pre_ffn": g_pre_ffn[l][None], "g_post_ffn": g_post_ffn[l][None],
        "w_in": w_in[l].astype(BF16), "g_sgu": g_sgu[l][None],
        "w_s": w_s[l].astype(BF16), "b_s_t": b_s[l].T,
        "g_subln_col": g_subln[l][:, None],
        "w_a": w_a[l].astype(BF16), "w_b": w_b[l].astype(BF16), "w_o": w_o[l].astype(BF16),
        "w_up": w_up[l].astype(BF16), "conv_w": conv_w[l], "conv_b": conv_b[l][None],
        "w_down": w_down[l].astype(BF16),
    }

    cond = jnp.concatenate([c_ctx[None], c, jnp.zeros((8 - 1 - ns, D_MODEL), F32)], axis=0)
    mod, lam = _ada_call(cond, w_ada[l], b_ada[l][None], lam_q1[l][None], lam_k1[l][None], lam_q2[l][None],
                         lam_k2[l][None], lam_init)
    mod = mod.reshape(8, 6, D_MODEL)
    mod_ctx, mod_s = mod[0:1], mod[1:1 + ns]

    yp, kv = _group(x_prompt, mod_ctx, p, lam, None, None, None, lam_init,
                    tm_proj=512, tm_mix=512, tm_ffn=512, tq=256, kc=256, emit_kv=True)
    past = cache_k.shape[2]
    ctx_k = cache_k.reshape(ns, past * N_HEADS_B, 2 * HEAD_DIM_B)
    ctx_v = cache_v.reshape(ns, past * N_HEADS_B, V_DIM_B)
    ys, _ = _group(x_sample, mod_s, p, lam, _rope_tables(t_s), ctx_k, ctx_v, lam_init,
                   tm_proj=512, tm_mix=512, tm_ffn=512, tq=256, kc=512, emit_kv=False)

    seq = x_prompt.shape[1]
    new_k = kv[0].reshape(nb, 1, seq, N_HEADS_B, 2 * HEAD_DIM_B)
    new_v = kv[1].reshape(nb, 1, seq, N_HEADS_B, V_DIM_B)
    return (yp, ys, new_k, new_v)
```

```python
import functools
import math

import jax
import jax.numpy as jnp
from jax import lax
from jax.experimental import pallas as pl
from jax.experimental.pallas import tpu as pltpu

D_MODEL = 1024
GRID_W = 64
CHUNK = 128
N_GROUPS_A = 4
D_A = 512
N_HEADS_B = 8
HEAD_DIM_B = 64
V_DIM_B = 2 * HEAD_DIM_B
D_B = N_HEADS_B * V_DIM_B
D_QK = N_HEADS_B * 2 * HEAD_DIM_B
D_FF = 2816
ROPE_THETA = 10000.0
EPS = 1e-6
D_IN = 2 * D_A + 2 * D_QK + D_B + 2 * D_MODEL
OFF_U, OFF_VA, OFF_Q, OFF_K, OFF_V, OFF_GA, OFF_GB = 0, D_A, 2 * D_A, 2 * D_A + D_QK, 2 * D_A + 2 * D_QK, \
    2 * D_A + 2 * D_QK + D_B, 2 * D_A + 2 * D_QK + D_B + D_MODEL

Q_SCALE = HEAD_DIM_B ** -0.5 * math.log2(math.e)

LANES = 128
BF16_SUBLANES = 16
VMEM_LIMIT_BYTES = 56 * 1024 * 1024

BF16 = jnp.bfloat16
F32 = jnp.float32


def _lambda_init(layer_idx):
    return 0.8 - 0.6 * math.exp(-0.3 * layer_idx)


def _rms(x, g):
    r = lax.rsqrt(jnp.mean(x * x, axis=-1, keepdims=True) + EPS)
    return (x * r) * g


def _gelu(x):
    c = 2.0 * math.sqrt(2.0 / math.pi)
    return x * jax.nn.sigmoid(x * (c + (c * 0.044715) * (x * x)))


def _dot(a, b):
    return jnp.dot(a, b, preferred_element_type=F32)


def _const_spec(shape):
    nd = len(shape)
    return pl.BlockSpec(shape, lambda *_: (0,) * nd, pipeline_mode=pl.Buffered(1))


def _mod_spec(mod, tm, seq):
    if mod.shape[0] == 1:
        return pl.BlockSpec((None, 6, D_MODEL), lambda i: (0, 0, 0))
    tiles_per_seq = seq // tm
    return pl.BlockSpec((None, 6, D_MODEL), lambda i: (i // tiles_per_seq, 0, 0))


def _params(n_axes):
    return pltpu.CompilerParams(dimension_semantics=("arbitrary",) * n_axes,
                                vmem_limit_bytes=VMEM_LIMIT_BYTES)


def _ada_kernel(cond_ref, w_ref, b_ref, lq1_ref, lk1_ref, lq2_ref, lk2_ref, mod_ref, lam_ref, *, lam_init):
    c = cond_ref[...]
    s = (c * jax.nn.sigmoid(c)).astype(BF16)
    mod_ref[...] = _dot(s, w_ref[...].astype(BF16)) + b_ref[...]
    d1 = jnp.sum(lq1_ref[...] * lk1_ref[...], axis=-1, keepdims=True)
    d2 = jnp.sum(lq2_ref[...] * lk2_ref[...], axis=-1, keepdims=True)
    lam = jnp.exp(d1) - jnp.exp(d2) + lam_init
    lam_ref[...] = jnp.broadcast_to(lam, lam_ref.shape)


def _ada_call(cond, w_ada, b_ada, lq1, lk1, lq2, lk2, lam_init):
    rows = cond.shape[0]
    tn = D_MODEL
    vec = lambda n: pl.BlockSpec((1, n), lambda j: (0, 0))
    return pl.pallas_call(
        functools.partial(_ada_kernel, lam_init=lam_init),
        grid=(6 * D_MODEL // tn,),
        in_specs=[pl.BlockSpec((rows, D_MODEL), lambda j: (0, 0)),
                  pl.BlockSpec((D_MODEL, tn), lambda j: (0, j)),
                  pl.BlockSpec((1, tn), lambda j: (0, j)),
                  vec(HEAD_DIM_B), vec(HEAD_DIM_B), vec(HEAD_DIM_B), vec(HEAD_DIM_B)],
        out_specs=[pl.BlockSpec((rows, tn), lambda j: (0, j)),
                   pl.BlockSpec((8, LANES), lambda j: (0, 0))],
        out_shape=[jax.ShapeDtypeStruct((rows, 6 * D_MODEL), F32),
                   jax.ShapeDtypeStruct((8, LANES), F32)],
        compiler_params=_params(1),
        name="ada",
    )(cond, w_ada, b_ada, lq1, lk1, lq2, lk2)


def _inproj_kernel(*refs, tm, rope, emit_kv):
    refs = list(refs)
    x_ref, mod_ref, gpre_ref, w_ref, gsgu_ref, ws_ref, bs_ref, wa_ref = refs[:8]
    pos = 8
    if rope:
        cos_ref, sin_ref = refs[pos:pos + 2]
        pos += 2
    gta_ref, q_ref, k_ref, v_ref, sgb_ref = refs[pos:pos + 5]
    pos += 5
    if emit_kv:
        kf_ref, vf_ref = refs[pos:pos + 2]
        pos += 2
    h_scr, a_scr = refs[pos:pos + 2]

    mod = mod_ref[...]
    sh1, sc1 = mod[0:1], mod[1:2]
    h_scr[...] = (_rms(x_ref[...], gpre_ref[...]) * (1.0 + sc1) + sh1).astype(BF16)

    def proj(off, width):
        return _dot(h_scr[...], w_ref[:, off:off + width])

    if rope:
        cos_t, sin_t = cos_ref[...], sin_ref[...]
        lane = lax.broadcasted_iota(jnp.int32, (tm, LANES), 1)
        first_half = (lane & (HEAD_DIM_B // 2)) == 0

    def rotary(xh):
        partner = jnp.where(first_half, pltpu.roll(xh, LANES - HEAD_DIM_B // 2, 1),
                            pltpu.roll(xh, HEAD_DIM_B // 2, 1))
        return xh * cos_t + partner * sin_t

    v = proj(OFF_V, D_B)
    if emit_kv:
        vf_ref[...] = v
    v_ref[...] = v.astype(BF16)

    ug = _gelu(proj(OFF_U, D_A))

    q = proj(OFF_Q, D_QK)
    for hb in range(N_HEADS_B):
        cols = slice(hb * LANES, (hb + 1) * LANES)
        qh = q[:, cols]
        if rope:
            qh = rotary(qh)
        q_ref[:, cols] = (qh * Q_SCALE).astype(BF16)

    vn = _rms(_gelu(proj(OFF_VA, D_A)), gsgu_ref[...]).astype(BF16)

    k = proj(OFF_K, D_QK)
    if emit_kv:
        kf_ref[...] = k
    for hb in range(N_HEADS_B):
        cols = slice(hb * LANES, (hb + 1) * LANES)
        kh = k[:, cols]
        if rope:
            kh = rotary(kh)
        k_ref[:, cols] = kh.astype(BF16)

    for j in range(tm // CHUNK):
        rows = slice(j * CHUNK, (j + 1) * CHUNK)
        for g in range(N_GROUPS_A):
            cols = slice(g * LANES, (g + 1) * LANES)
            mixed = _dot(ws_ref[g], vn[rows, cols]) + bs_ref[:, g:g + 1]
            a_scr[rows, cols] = (ug[rows, cols] * mixed).astype(BF16)

    sgb_ref[...] = jax.nn.sigmoid(proj(OFF_GB, D_MODEL))
    gta_ref[...] = jax.nn.sigmoid(proj(OFF_GA, D_MODEL)) * _dot(a_scr[...], wa_ref[...])


def _inproj_call(x, mod, g_pre, w_in, g_sgu, w_s, b_s_t, w_a, rope_tabs, *, seq, tm, emit_kv):
    n = x.shape[0]
    tiles_per_seq = seq // tm
    rope = rope_tabs is not None
    row = lambda w: pl.BlockSpec((tm, w), lambda i: (i, 0))
    in_specs = [row(D_MODEL), _mod_spec(mod, tm, seq),
                _const_spec((1, D_MODEL)), _const_spec((D_MODEL, D_IN)), _const_spec((1, D_A)),
                _const_spec((N_GROUPS_A, CHUNK, CHUNK)), _const_spec((CHUNK, N_GROUPS_A)),
                _const_spec((D_A, D_MODEL))]
    args = [x, mod, g_pre, w_in, g_sgu, w_s, b_s_t, w_a]
    if rope:
        tab = pl.BlockSpec((tm, LANES), lambda i: (i % tiles_per_seq, 0))
        in_specs += [tab, tab]
        args += list(rope_tabs)
    widths = [(D_MODEL, F32), (D_QK, BF16), (D_QK, BF16), (D_B, BF16), (D_MODEL, F32)]
    if emit_kv:
        widths += [(D_QK, F32), (D_B, F32)]
    return pl.pallas_call(
        functools.partial(_inproj_kernel, tm=tm, rope=rope, emit_kv=emit_kv),
        grid=(n // tm,),
        in_specs=in_specs,
        out_specs=[row(w) for w, _ in widths],
        out_shape=[jax.ShapeDtypeStruct((n, w), dt) for w, dt in widths],
        scratch_shapes=[pltpu.VMEM((tm, D_MODEL), BF16), pltpu.VMEM((tm, D_A), BF16)],
        compiler_params=_params(1),
        name="inproj",
    )(*args)


def _two_map_queries(q):
    lane = lax.broadcasted_iota(jnp.int32, q.shape, 1)
    zero = jnp.zeros_like(q)
    return jnp.concatenate([jnp.where(lane < HEAD_DIM_B, q, zero), jnp.where(lane >= HEAD_DIM_B, q, zero)], axis=0)


def _scores_t(k, qz):
    return lax.dot_general(k, qz, (((1,), (1,)), ((), ())), preferred_element_type=F32)


def _diff_head_out(acc, l, lam, g_col, tq, lam_init):
    rl = 1.0 / l
    o = acc[:, :tq] * rl[:, :tq] - lam * (acc[:, tq:] * rl[:, tq:])
    r = lax.rsqrt(jnp.mean(o * o, axis=0, keepdims=True) + EPS)
    y = (o * r) * g_col * (1.0 - lam_init)
    return y.T.astype(BF16)


def _attn_kernel(qa_ref, qb_ref, k_ref, v_ref, ck_ref, cv_ref, lam_ref, gsub_ref, o_ref, kall, vt, s_even, s_odd,
                 mbuf, *, tq, kc, n_ctx, n_own, n_q, n_tiles, tiles_per_step, lam_init):
    p_tiles = tiles_per_step
    n_chunks = n_ctx + n_own
    i = pl.program_id(0)
    newest = p_tiles * i

    @pl.when(jnp.logical_and(newest % n_q == 0, newest < n_tiles))
    def _stage_keys():
        slot = (newest // n_q) % 2
        head = (newest // n_q) % N_HEADS_B
        for c in range(n_ctx):
            rows = pl.ds(c * kc * N_HEADS_B + head, kc, stride=N_HEADS_B)
            kall[slot, c] = ck_ref[rows, :].astype(BF16)
            vt[slot, c] = cv_ref[rows, :].T.astype(BF16)
        for c in range(n_own):
            rows = slice(c * kc, (c + 1) * kc)
            kall[slot, n_ctx + c] = k_ref[rows, :]
            vt[slot, n_ctx + c] = v_ref[rows, :].astype(F32).T.astype(BF16)

    @pl.when(i == 0)
    def _define_first_read():
        s_even[...] = jnp.zeros(s_even.shape, F32)
        mbuf[0] = jnp.zeros(mbuf.shape[1:], F32)

    s_bufs = (s_even, s_odd)
    lam = lam_ref[0:1, 0:1]
    g_col = gsub_ref[...]
    first = p_tiles * (i - 1)
    for u in range(p_tiles):
        k_slot = (jnp.clip(first + u + 1, 0, n_tiles - 1) // n_q) % 2
        v_slot = (jnp.clip(first + u, 0, n_tiles - 1) // n_q) % 2
        q = qa_ref[(u + 1) * tq:(u + 2) * tq, :] if u < p_tiles - 1 else qb_ref[...]
        qz = _two_map_queries(q)
        s_cur, s_prev = s_bufs[(u + 1) % 2], s_bufs[u % 2]
        m_prev = mbuf[u % 2]
        m = jnp.full((1, 2 * tq), -jnp.inf, F32)
        l = jnp.zeros((1, 2 * tq), F32)
        acc = jnp.zeros((V_DIM_B, 2 * tq), F32)
        for c in range(n_chunks):
            s = _scores_t(kall[k_slot, c], qz)
            s_cur[c] = s
            m = jnp.maximum(m, jnp.max(s, axis=0, keepdims=True))
            p = jnp.exp2(s_prev[c] - m_prev)
            l = l + jnp.sum(p, axis=0, keepdims=True)
            acc = acc + _dot(vt[v_slot, c], p.astype(BF16))
        mbuf[(u + 1) % 2] = m
        o_ref[u * tq:(u + 1) * tq, :] = _diff_head_out(acc, l, lam, g_col, tq, lam_init)


def _attn_call(q, k, v, ctx_k, ctx_v, lam, g_sub_col, *, tq, kc, tiles_per_step, lam_init):
    b, t, _ = q.shape
    n_q = t // tq
    n_tiles = b * N_HEADS_B * n_q
    p_tiles = tiles_per_step
    assert p_tiles % 2 == 0 and n_q % p_tiles == 0
    n_own = t // kc
    ctx_rows = ctx_k.shape[1]
    n_ctx = ctx_rows // N_HEADS_B // kc
    n_chunks = n_own + n_ctx

    def head_of(tile):
        g = tile // n_q
        return g // N_HEADS_B, g % N_HEADS_B

    def group_map(i):
        tile = p_tiles * jnp.maximum(i - 1, 0)
        bi, h = head_of(tile)
        return bi, (tile % n_q) // p_tiles, h

    def newest_map(i):
        tile = jnp.minimum(p_tiles * i, n_tiles - 1)
        bi, h = head_of(tile)
        return bi, tile % n_q, h

    def head_map(i):
        bi, h = head_of(jnp.minimum(p_tiles * i, n_tiles - 1))
        return bi, 0, h

    def batch_map(i):
        return head_map(i)[0], 0, 0

    head_rows = lambda rows: pl.BlockSpec((None, rows, LANES), head_map)
    ctx_spec = pl.BlockSpec((None, ctx_rows, LANES), batch_map)
    group_spec = pl.BlockSpec((None, p_tiles * tq, LANES), group_map)
    return pl.pallas_call(
        functools.partial(_attn_kernel, tq=tq, kc=kc, n_ctx=n_ctx, n_own=n_own, n_q=n_q, n_tiles=n_tiles,
                          tiles_per_step=p_tiles, lam_init=lam_init),
        grid=(n_tiles // p_tiles + 1,),
        in_specs=[group_spec, pl.BlockSpec((None, tq, LANES), newest_map),
                  head_rows(t), head_rows(t), ctx_spec, ctx_spec,
                  pl.BlockSpec((8, LANES), lambda i: (0, 0)),
                  pl.BlockSpec((V_DIM_B, 1), lambda i: (0, 0))],
        out_specs=group_spec,
        out_shape=jax.ShapeDtypeStruct((b, t, D_B), BF16),
        scratch_shapes=[pltpu.VMEM((2, n_chunks, kc, LANES), BF16),
                        pltpu.VMEM((2, n_chunks, V_DIM_B, kc), BF16),
                        pltpu.VMEM((n_chunks, kc, 2 * tq), F32),
                        pltpu.VMEM((n_chunks, kc, 2 * tq), F32),
                        pltpu.VMEM((2, 1, 2 * tq), F32)],
        compiler_params=_params(1),
        name="attn",
    )(q, q, k, v, ctx_k, ctx_v, lam, g_sub_col)


def _attn_seq_kernel(q_ref, k_ref, v_ref, lam_ref, gsub_ref, gta_ref, sgb_ref, x_ref, mod_ref, wb_ref, wo_ref,
                     gpost_ref, x1_ref, o_scr, *, t, n_seq, lam_init):
    lam = lam_ref[0:1, 0:1]
    g_col = gsub_ref[...]
    heads = [slice(h * LANES, (h + 1) * LANES) for h in range(N_HEADS_B)]
    g1 = mod_ref[2:3, :]
    for b in range(n_seq):
        s = [_scores_t(k_ref[b, :, cols], _two_map_queries(q_ref[b, :, cols])) for cols in heads]
        v_t = [v_ref[b, :, cols].astype(F32).T.astype(BF16) for cols in heads]
        p = [jnp.exp2(sh - jnp.max(sh, axis=0, keepdims=True)) for sh in s]
        l = [jnp.sum(ph, axis=0, keepdims=True) for ph in p]
        acc = [_dot(vh, ph.astype(BF16)) for vh, ph in zip(v_t, p)]
        for cols, acc_h, l_h in zip(heads, acc, l):
            o_scr[b, :, cols] = _diff_head_out(acc_h, l_h, lam, g_col, t, lam_init)
        merged = gta_ref[b] + sgb_ref[b] * _dot(o_scr[b], wb_ref[...])
        mix = _dot(merged.astype(BF16), wo_ref[...])
        x1_ref[b] = x_ref[b] + g1 * _rms(mix, gpost_ref[...])


def _attn_seq_call(q, k, v, lam, g_sub_col, gta, sgb, x, mod, w_b, w_o, g_post, *, n_seq, lam_init):
    b, t, _ = q.shape
    seq_block = pl.BlockSpec((n_seq, t, D_B), lambda bi: (bi, 0, 0))
    row_block = pl.BlockSpec((n_seq, t, D_MODEL), lambda bi: (bi, 0, 0))
    assert mod.shape[0] == 1, "one shared modulation row set"
    return pl.pallas_call(
        functools.partial(_attn_seq_kernel, t=t, n_seq=n_seq, lam_init=lam_init),
        grid=(b // n_seq,),
        in_specs=[seq_block, seq_block, seq_block,
                  pl.BlockSpec((8, LANES), lambda bi: (0, 0)),
                  pl.BlockSpec((V_DIM_B, 1), lambda bi: (0, 0)),
                  row_block, row_block, row_block,
                  pl.BlockSpec((None, 6, D_MODEL), lambda bi: (0, 0, 0)),
                  _const_spec((D_B, D_MODEL)), _const_spec((D_MODEL, D_MODEL)), _const_spec((1, D_MODEL))],
        out_specs=row_block,
        out_shape=jax.ShapeDtypeStruct((b, t, D_MODEL), F32),
        scratch_shapes=[pltpu.VMEM((n_seq, t, D_B), BF16)],
        compiler_params=_params(1),
        name="attn_seq",
    )(q, k, v, lam, g_sub_col, gta, sgb, x, mod, w_b, w_o, g_post)


def _mix_kernel(gta_ref, o_ref, sgb_ref, x_ref, mod_ref, wb_ref, wo_ref, gpost_ref, x1_ref):
    merged = gta_ref[...] + sgb_ref[...] * _dot(o_ref[...], wb_ref[...])
    mix = _dot(merged.astype(BF16), wo_ref[...])
    g1 = mod_ref[2:3, :]
    x1_ref[...] = x_ref[...] + g1 * _rms(mix, gpost_ref[...])


def _mix_call(gta, o, sgb, x, mod, w_b, w_o, g_post, *, seq, tm):
    n = x.shape[0]
    row = lambda w: pl.BlockSpec((tm, w), lambda i: (i, 0))
    return pl.pallas_call(
        _mix_kernel,
        grid=(n // tm,),
        in_specs=[row(D_MODEL), row(D_B), row(D_MODEL), row(D_MODEL), _mod_spec(mod, tm, seq),
                  _const_spec((D_B, D_MODEL)), _const_spec((D_MODEL, D_MODEL)), _const_spec((1, D_MODEL))],
        out_specs=row(D_MODEL),
        out_shape=jax.ShapeDtypeStruct((n, D_MODEL), F32),
        compiler_params=_params(1),
        name="mix",
    )(gta, o, sgb, x, mod, w_b, w_o, g_post)


FFN_HALO = BF16_SUBLANES


def _ffn_kernel(*refs, tm, seq, halo, n_col_chunks):
    refs = list(refs)
    x_ref = refs[0]
    pos = 1
    if halo:
        xp_ref, xn_ref = refs[1:3]
        pos = 3
    mod_ref, gpre_ref, wup_ref, cw_ref, cb_ref, wdn_ref, gpost_ref, out_ref, hbuf = refs[pos:pos + 9]

    mod = mod_ref[...]
    sh2, sc2, g2 = mod[3:4], mod[4:5], mod[5:6]

    def pre(xv):
        return (_rms(xv, gpre_ref[...]) * (1.0 + sc2) + sh2).astype(BF16)

    x = x_ref[...]
    lo = FFN_HALO if halo else 0
    rows_all = tm + 2 * lo
    row0 = pl.program_id(0) * tm
    if halo:
        zero_h = jnp.zeros((lo, D_MODEL), BF16)
        hbuf[0:lo] = jnp.where(row0 % seq == 0, zero_h, pre(xp_ref[...]))
        hbuf[lo + tm:rows_all] = jnp.where((row0 + tm) % seq == 0, zero_h, pre(xn_ref[...]))
    else:
        t_in_seq = (row0 + lax.broadcasted_iota(jnp.int32, (tm, 1), 0)) & (seq - 1)
        is_first = t_in_seq == 0
        is_last = t_in_seq == seq - 1
    hbuf[lo:lo + tm] = pre(x)

    cw_chunk = D_FF // n_col_chunks
    acc = None
    for c in range(n_col_chunks):
        ca = c * cw_chunk
        a = _dot(hbuf[...], wup_ref[:, ca:ca + cw_chunk])
        gate = _dot(hbuf[lo:lo + tm], wup_ref[:, D_FF + ca:D_FF + ca + cw_chunk])
        a_prev = pltpu.roll(a, 1, 0)[lo:lo + tm]
        a_next = pltpu.roll(a, rows_all - 1, 0)[lo:lo + tm]
        a_mid = a[lo:lo + tm]
        if not halo:
            a_prev = jnp.where(is_first, 0.0, a_prev)
            a_next = jnp.where(is_last, 0.0, a_next)
        cw = cw_ref[:, ca:ca + cw_chunk]
        conv = cw[0:1] * a_prev + cw[1:2] * a_mid + cw[2:3] * a_next + cb_ref[:, ca:ca + cw_chunk]
        act = (_gelu(conv) * gate).astype(BF16)
        y = _dot(act, wdn_ref[ca:ca + cw_chunk, :])
        acc = y if acc is None else acc + y
    out_ref[...] = x + g2 * _rms(acc, gpost_ref[...])


def _ffn_call(x1, mod, g_pre, w_up, conv_w, conv_b, w_down, g_post, *, seq, tm, n_col_chunks):
    n = x1.shape[0]
    halo = tm % seq != 0
    row = pl.BlockSpec((tm, D_MODEL), lambda i: (i, 0))
    in_specs = [row]
    args = [x1]
    if halo:
        hb = tm // FFN_HALO
        last = n // FFN_HALO - 1
        in_specs += [pl.BlockSpec((FFN_HALO, D_MODEL), lambda i: (jnp.maximum(i * hb - 1, 0), 0)),
                     pl.BlockSpec((FFN_HALO, D_MODEL), lambda i: (jnp.minimum((i + 1) * hb, last), 0))]
        args += [x1, x1]
    in_specs += [_mod_spec(mod, tm, seq),
                 _const_spec((1, D_MODEL)), _const_spec((D_MODEL, 2 * D_FF)), _const_spec((3, D_FF)),
                 _const_spec((1, D_FF)), _const_spec((D_FF, D_MODEL)), _const_spec((1, D_MODEL))]
    args += [mod, g_pre, w_up, conv_w, conv_b, w_down, g_post]
    rows_all = tm + (2 * FFN_HALO if halo else 0)
    return pl.pallas_call(
        functools.partial(_ffn_kernel, tm=tm, seq=seq, halo=halo, n_col_chunks=n_col_chunks),
        grid=(n // tm,),
        in_specs=in_specs,
        out_specs=row,
        out_shape=jax.ShapeDtypeStruct((n, D_MODEL), F32),
        scratch_shapes=[pltpu.VMEM((rows_all, D_MODEL), BF16)],
        compiler_params=_params(1),
        name="ffn",
    )(*args)


def _rope_tables(n_tok):
    pos = jnp.arange(n_tok)
    row = (pos // GRID_W).astype(F32)
    col = (pos % GRID_W).astype(F32)
    n_freq = HEAD_DIM_B // 4
    inv = ROPE_THETA ** (-jnp.arange(n_freq, dtype=F32) / n_freq)
    ang = jnp.concatenate([row[:, None] * inv, col[:, None] * inv], axis=-1)
    cos, sin = jnp.cos(ang), jnp.sin(ang)
    cos_t = jnp.tile(cos, (1, LANES // (HEAD_DIM_B // 2)))
    sin_t = jnp.tile(jnp.concatenate([-sin, sin], axis=-1), (1, LANES // HEAD_DIM_B))
    return cos_t, sin_t


def _group(x, mod, p, lam, rope_tabs, ctx_k, ctx_v, lam_init, *, tm_proj, tm_mix, tm_ffn, tq, kc, emit_kv):
    b, t, _ = x.shape
    xf = x.reshape(b * t, D_MODEL)
    outs = _inproj_call(xf, mod, p["g_pre_mix"], p["w_in"], p["g_sgu"], p["w_s"], p["b_s_t"], p["w_a"], rope_tabs,
                        seq=t, tm=tm_proj, emit_kv=emit_kv)
    gta, q, k, v, sgb = outs[:5]
    q, k, v = q.reshape(b, t, D_QK), k.reshape(b, t, D_QK), v.reshape(b, t, D_B)
    if ctx_k is None:
        as_seqs = lambda a: a.reshape(b, t, D_MODEL)
        x1 = _attn_seq_call(q, k, v, lam, p["g_subln_col"], as_seqs(gta), as_seqs(sgb), x, mod, p["w_b"], p["w_o"],
                            p["g_post_mix"], n_seq=2, lam_init=lam_init).reshape(b * t, D_MODEL)
    else:
        o = _attn_call(q, k, v, ctx_k, ctx_v, lam, p["g_subln_col"], tq=tq, kc=kc, tiles_per_step=4,
                       lam_init=lam_init)
        x1 = _mix_call(gta, o.reshape(b * t, D_B), sgb, xf, mod, p["w_b"], p["w_o"], p["g_post_mix"], seq=t,
                       tm=tm_mix)
    y = _ffn_call(x1, mod, p["g_pre_ffn"], p["w_up"], p["conv_w"], p["conv_b"], p["w_down"], p["g_post_ffn"],
                  seq=t, tm=tm_ffn, n_col_chunks=1)
    kv = outs[5:] if emit_kv else None
    return y.reshape(b, t, D_MODEL), kv


def kernel(x_prompt, x_sample, c, cache_k, cache_v, c_ctx, w_ada, b_ada, g_pre_mix, g_post_mix, g_pre_ffn,
           g_post_ffn, w_in, g_sgu, w_s, b_s, lam_q1, lam_k1, lam_q2, lam_k2, g_subln, w_a, w_b, w_o, w_up,
           conv_w, conv_b, w_down):
    depth = w_in.shape[0]
    assert depth == 1, "single-layer configuration"
    l = 0
    lam_init = _lambda_init(l)
    nb, ns = x_prompt.shape[0], x_sample.shape[0]
    t_s = x_sample.shape[1]

    p = {
        "g_pre_mix": g_pre_mix[l][None], "g_post_mix": g_post_mix[l][None],
        "g_pre_ffn": g_pre_ffn[l][None], "g_post_ffn": g_post_ffn[l][None],
        "w_in": w_in[l].astype(BF16), "g_sgu": g_sgu[l][None],
        "w_s": w_s[l].astype(BF16), "b_s_t": b_s[l].T,
        "g_subln_col": g_subln[l][:, None],
        "w_a": w_a[l].astype(BF16), "w_b": w_b[l].astype(BF16), "w_o": w_o[l].astype(BF16),
        "w_up": w_up[l].astype(BF16), "conv_w": conv_w[l], "conv_b": conv_b[l][None],
        "w_down": w_down[l].astype(BF16),
    }

    cond = jnp.concatenate([c_ctx[None], c, jnp.zeros((8 - 1 - ns, D_MODEL), F32)], axis=0)
    mod, lam = _ada_call(cond, w_ada[l], b_ada[l][None], lam_q1[l][None], lam_k1[l][None], lam_q2[l][None],
                         lam_k2[l][None], lam_init)
    mod = mod.reshape(8, 6, D_MODEL)
    mod_ctx, mod_s = mod[0:1], mod[1:1 + ns]

    yp, kv = _group(x_prompt, mod_ctx, p, lam, None, None, None, lam_init,
                    tm_proj=512, tm_mix=512, tm_ffn=512, tq=256, kc=256, emit_kv=True)
    past = cache_k.shape[2]
    ctx_k = cache_k.reshape(ns, past * N_HEADS_B, 2 * HEAD_DIM_B)
    ctx_v = cache_v.reshape(ns, past * N_HEADS_B, V_DIM_B)
    ys, _ = _group(x_sample, mod_s, p, lam, _rope_tables(t_s), ctx_k, ctx_v, lam_init,
                   tm_proj=512, tm_mix=512, tm_ffn=512, tq=256, kc=512, emit_kv=False)

    seq = x_prompt.shape[1]
    new_k = kv[0].reshape(nb, 1, seq, N_HEADS_B, 2 * HEAD_DIM_B)
    new_v = kv[1].reshape(nb, 1, seq, N_HEADS_B, V_DIM_B)
    return (yp, ys, new_k, new_v)
```
